```python
import math
import jax, jax.numpy as jnp
from jax import lax
import numpy as np

D_MODEL = 2048
BATCH = 32
SEQ = 256
DEPTH = 2
DEC_BATCH = 4
DEC_SEQ = 4096
PAST_LEN = 256

GRID_W = 64
N_MIXERS = 2
N_SSM_LAYERS = (DEPTH + 1) // 2
N_ATTN_LAYERS = DEPTH // 2
GROUP_CH = 16
N_GROUPS = D_MODEL // GROUP_CH
STATE_P = 64
STEP_MIN = 1e-3
STEP_MAX = 1e-1
HEAD_DIM = 64
N_HEADS = D_MODEL // HEAD_DIM
N_KV_HEADS = 4
Q_PER_KV = N_HEADS // N_KV_HEADS
WINDOW = 128
ATTN_BLOCK = 128
ROPE_BASE = 10000.0
ROPE_PAIRS = HEAD_DIM // 4
N_EXPERTS = 256
TOP_K = 8
N_EXPERT_GROUPS = 8
TOPK_GROUPS = 4
D_EXPERT = 512
ROUTED_SCALE = 2.5
EXPERT_BLOCK = 128
RMS_EPS = 1e-6
NEG_INF = -1e30

kernel_name = 'hybrid_s5_swa_moe_diffusion_step'

F32 = jnp.float32


def _rms_norm(x, g):
    xf = x.astype(F32)
    y = xf * lax.rsqrt(jnp.mean(xf * xf, axis=-1, keepdims=True) + RMS_EPS)
    return (y * g.astype(F32)).astype(x.dtype)


def _ada_params(cvec, w_ada, b_ada):
    m = jax.nn.silu(cvec) @ w_ada + b_ada
    return jnp.split(m[:, None, :], 6, axis=-1)


def _swiglu(x, w_gate, w_up, w_down):
    return (jax.nn.silu(x @ w_gate) * (x @ w_up)) @ w_down


def _ssm_combine(e1, e2):
    a1, b1 = e1
    a2, b2 = e2
    return a1 * a2, a2 * b1 + b2


def _s5_direction(u, h0, lam_re, lam_im, log_step, b_re, b_im, c_re, c_im):
    lam = lax.complex(jnp.minimum(lam_re.astype(F32), -1e-4), lam_im.astype(F32))
    step = jnp.exp(log_step.astype(F32))[:, None]
    lam_bar = jnp.exp(lam * step)
    b_bar = ((lam_bar - 1.0) / lam)[..., None] * lax.complex(b_re.astype(F32), b_im.astype(F32))
    c_mat = lax.complex(c_re.astype(F32), c_im.astype(F32))
    bu = jnp.einsum('blgh,gph->blgp', u.astype(jnp.complex64), b_bar)
    bu = bu.at[:, 0].add(lam_bar * h0)
    a = jnp.broadcast_to(lam_bar, (1,) + bu.shape[1:])
    _, h = lax.associative_scan(_ssm_combine, (a, bu), axis=1)
    y = jnp.real(jnp.einsum('blgp,ghp->blgh', h, c_mat))
    return y, h[:, -1]


def _s5_mixer(h, h0_re, h0_im, lam_re, lam_im, log_step, b_re, b_im, c_re, c_im, d, w_glu):
    bsz, L, dm = h.shape
    u = h.astype(F32).reshape(bsz, L, N_GROUPS, GROUP_CH)
    h0 = lax.complex(h0_re.astype(F32), h0_im.astype(F32))
    y_f, hT_f = _s5_direction(u, h0[:, 0], lam_re[0], lam_im[0], log_step[0],
                              b_re[0], b_im[0], c_re[0], c_im[0])
    y_b, hT_b = _s5_direction(jnp.flip(u, 1), h0[:, 1], lam_re[1], lam_im[1], log_step[1],
                              b_re[1], b_im[1], c_re[1], c_im[1])
    y = y_f + jnp.flip(y_b, 1) + d.astype(F32).reshape(N_GROUPS, GROUP_CH) * u
    g = jax.nn.gelu(y.reshape(bsz, L, dm)).astype(h.dtype)
    val, gate = jnp.split(g @ w_glu, 2, axis=-1)
    out = val * jax.nn.sigmoid(gate)
    hT = jnp.stack([hT_f, hT_b], axis=1)
    return out, jnp.real(hT), jnp.imag(hT)


def _axial_rope(x):
    L = x.shape[1]
    rows = L // GRID_W
    row = jnp.repeat(jnp.arange(rows), GRID_W)
    col = jnp.tile(jnp.arange(GRID_W), rows)
    inv_freq = ROPE_BASE ** (-jnp.arange(ROPE_PAIRS, dtype=F32) / ROPE_PAIRS)
    ang = jnp.stack([row, col], -1).astype(F32)[..., None] * inv_freq
    bshape = (1, L) + (1,) * (x.ndim - 3) + (2, ROPE_PAIRS)
    cos = jnp.cos(ang).reshape(bshape)
    sin = jnp.sin(ang).reshape(bshape)
    xr = x.astype(F32).reshape(x.shape[:-1] + (2, 2, ROPE_PAIRS))
    x1, x2 = xr[..., 0, :], xr[..., 1, :]
    out = jnp.stack([x1 * cos - x2 * sin, x2 * cos + x1 * sin], axis=-2)
    return out.reshape(x.shape).astype(x.dtype)


def _split_qkv(h, w_qkv, b_qkv):
    bsz, L, _ = h.shape
    qkv = h @ w_qkv + b_qkv
    nq = N_HEADS * HEAD_DIM
    nk = N_KV_HEADS * HEAD_DIM
    q = qkv[..., :nq].reshape(bsz, L, N_KV_HEADS, Q_PER_KV, HEAD_DIM)
    k = qkv[..., nq:nq + nk].reshape(bsz, L, N_KV_HEADS, HEAD_DIM)
    v = qkv[..., nq + nk:].reshape(bsz, L, N_KV_HEADS, HEAD_DIM)
    return q, k, v


def _sink_attend(q, k, v, sink, mask):
    s = jnp.einsum('bqkgd,bskd->bkgqs', q, k).astype(F32) * (HEAD_DIM ** -0.5)
    if mask is not None:
        s = jnp.where(mask, s, NEG_INF)
    sink_col = jnp.broadcast_to(sink.astype(F32).reshape(1, N_KV_HEADS, Q_PER_KV, 1, 1),
                                s.shape[:-1] + (1,))
    p = jax.nn.softmax(jnp.concatenate([s, sink_col], axis=-1), axis=-1)[..., :-1]
    return jnp.einsum('bkgqs,bskd->bqkgd', p.astype(v.dtype), v)


def _attn_context(h, w_qkv, b_qkv, w_o, sink):
    bsz, L, _ = h.shape
    q, k, v = _split_qkv(h, w_qkv, b_qkv)
    nq = L // ATTN_BLOCK
    qb = jnp.moveaxis(q.reshape(bsz, nq, ATTN_BLOCK, N_KV_HEADS, Q_PER_KV, HEAD_DIM), 1, 0)
    o = lax.map(lambda qblk: _sink_attend(qblk, k, v, sink, None), qb)
    o = jnp.moveaxis(o, 0, 1).reshape(bsz, L, N_HEADS * HEAD_DIM)
    return o @ w_o, k, v


def _attn_latent(h, ctx_k, ctx_v, w_qkv, b_qkv, w_o, sink):
    bsz, L, _ = h.shape
    q, k, v = _split_qkv(h, w_qkv, b_qkv)
    q = _axial_rope(q)
    k = _axial_rope(k)
    nb = L // ATTN_BLOCK
    qb = jnp.moveaxis(q.reshape(bsz, nb, ATTN_BLOCK, N_KV_HEADS, Q_PER_KV, HEAD_DIM), 1, 0)

    def neighbours(t):
        tp = jnp.pad(t, ((0, 0), (ATTN_BLOCK, ATTN_BLOCK), (0, 0), (0, 0)))
        tp = tp.reshape(bsz, nb + 2, ATTN_BLOCK, N_KV_HEADS, HEAD_DIM)
        return jnp.moveaxis(jnp.concatenate([tp[:, :-2], tp[:, 1:-1], tp[:, 2:]], axis=2), 1, 0)

    kb, vb = neighbours(k), neighbours(v)
    qi = jnp.arange(ATTN_BLOCK)[:, None]
    sj = jnp.arange(3 * ATTN_BLOCK)[None, :]
    band = jnp.abs(sj - ATTN_BLOCK - qi) <= WINDOW
    kpos = jnp.arange(nb)[:, None, None] * ATTN_BLOCK - ATTN_BLOCK + sj
    mask = band[None] & (kpos >= 0) & (kpos < L)
    ctx_mask = jnp.ones((ATTN_BLOCK, ctx_k.shape[1]), bool)

    def one_block(args):
        qblk, kblk, vblk, m = args
        keys = jnp.concatenate([ctx_k.astype(kblk.dtype), kblk], axis=1)
        vals = jnp.concatenate([ctx_v.astype(vblk.dtype), vblk], axis=1)
        return _sink_attend(qblk, keys, vals, sink, jnp.concatenate([ctx_mask, m], axis=1))

    o = lax.map(one_block, (qb, kb, vb, mask))
    o = jnp.moveaxis(o, 0, 1).reshape(bsz, L, N_HEADS * HEAD_DIM)
    return o @ w_o


def _route(xf, w_router, bias):
    T = xf.shape[0]
    per_group = N_EXPERTS // N_EXPERT_GROUPS
    scores = jax.nn.sigmoid(xf.astype(F32) @ w_router.astype(F32))
    sel = scores + bias.astype(F32)
    grp_score = lax.top_k(sel.reshape(T, N_EXPERT_GROUPS, per_group), 2)[0].sum(-1)
    _, gidx = lax.top_k(grp_score, TOPK_GROUPS)
    gmask = jax.nn.one_hot(gidx, N_EXPERT_GROUPS, dtype=F32).sum(-2) > 0
    sel = jnp.where(jnp.repeat(gmask, per_group, axis=-1), sel, -jnp.inf)
    _, eidx = lax.top_k(sel, TOP_K)
    w = jnp.take_along_axis(scores, eidx, axis=-1)
    w = w / jnp.sum(w, axis=-1, keepdims=True) * ROUTED_SCALE
    return eidx, w


def _routed_experts(xf, eidx, ew, w_gate, w_up, w_down):
    T, dm = xf.shape
    A = T * TOP_K
    flat_e = eidx.reshape(A)
    order = jnp.argsort(flat_e)
    sorted_e = flat_e[order]
    counts = jnp.bincount(flat_e, length=N_EXPERTS)
    padded = (counts + EXPERT_BLOCK - 1) // EXPERT_BLOCK * EXPERT_BLOCK
    pad_end = jnp.cumsum(padded)
    pad_start = pad_end - padded
    grp_start = jnp.cumsum(counts) - counts
    slot = pad_start[sorted_e] + jnp.arange(A) - grp_start[sorted_e]
    n_blocks = -(-A // EXPERT_BLOCK) + N_EXPERTS
    n_slots = n_blocks * EXPERT_BLOCK
    slot_token = jnp.full((n_slots,), T, jnp.int32).at[slot].set((order // TOP_K).astype(jnp.int32))
    slot_w = jnp.zeros((n_slots,), F32).at[slot].set(ew.reshape(A).astype(F32)[order])
    block_expert = jnp.minimum(
        jnp.searchsorted(pad_end, jnp.arange(n_blocks) * EXPERT_BLOCK, side='right'), N_EXPERTS - 1)
    x_pad = jnp.concatenate([xf, jnp.zeros((1, dm), xf.dtype)], axis=0)

    def expert_block(args):
        tok, e, wt = args
        y = _swiglu(x_pad[tok], w_gate[e], w_up[e], w_down[e])
        return y * wt[:, None].astype(y.dtype)

    yb = lax.map(expert_block, (slot_token.reshape(n_blocks, EXPERT_BLOCK), block_expert,
                                slot_w.reshape(n_blocks, EXPERT_BLOCK)))
    out = jnp.zeros((T + 1, dm), yb.dtype).at[slot_token].add(yb.reshape(n_slots, dm))
    return out[:T]


def _moe(h, w_router, bias, w_gate, w_up, w_down, s_gate, s_up, s_down):
    bsz, L, dm = h.shape
    xf = h.reshape(bsz * L, dm)
    eidx, ew = _route(xf, w_router, bias)
    y = _routed_experts(xf, eidx, ew, w_gate, w_up, w_down) + _swiglu(xf, s_gate, s_up, s_down)
    return y.reshape(bsz, L, dm)


def setup_inputs(seed: int = 0) -> dict:
    key = jax.random.key(seed)
    ks = iter(jax.random.split(key, 48))

    def nrm(shape, scale):
        return jax.random.normal(next(ks), shape, F32) * scale

    D = D_MODEL
    qkv_dim = (N_HEADS + 2 * N_KV_HEADS) * HEAD_DIM
    ssm_shape = (N_SSM_LAYERS, 2, N_GROUPS, STATE_P)
    n_idx = jnp.arange(STATE_P, dtype=F32)
    return {
        'x_prompt': nrm((BATCH, SEQ, D), 1.0),
        'x_sample': nrm((DEC_BATCH, DEC_SEQ, D), 1.0),
        'c': nrm((DEC_BATCH, D), 1.0),
        'state_ssm_re': nrm((DEC_BATCH, N_SSM_LAYERS, 2, N_GROUPS, STATE_P), 0.5),
        'state_ssm_im': nrm((DEC_BATCH, N_SSM_LAYERS, 2, N_GROUPS, STATE_P), 0.5),
        'cache_k': nrm((DEC_BATCH, N_ATTN_LAYERS, PAST_LEN, N_KV_HEADS, HEAD_DIM), 1.0),
        'cache_v': nrm((DEC_BATCH, N_ATTN_LAYERS, PAST_LEN, N_KV_HEADS, HEAD_DIM), 1.0),
        'c_ctx': nrm((D,), 1.0),
        'ada_w': nrm((DEPTH, D, 6 * D), 0.5 * D ** -0.5),
        'ada_b': nrm((DEPTH, 6 * D), 0.01),
        'norm1_g': 1.0 + nrm((DEPTH, D), 0.02),
        'norm2_g': 1.0 + nrm((DEPTH, D), 0.02),
        'final_norm_g': 1.0 + nrm((D,), 0.02),
        's5_lam_re': -0.5 + nrm(ssm_shape, 0.01),
        's5_lam_im': math.pi * n_idx + nrm(ssm_shape, 0.01),
        's5_log_step': jax.random.uniform(next(ks), (N_SSM_LAYERS, 2, N_GROUPS), F32,
                                          math.log(STEP_MIN), math.log(STEP_MAX)),
        's5_b_re': nrm((N_SSM_LAYERS, 2, N_GROUPS, STATE_P, GROUP_CH), (2 * GROUP_CH) ** -0.5),
        's5_b_im': nrm((N_SSM_LAYERS, 2, N_GROUPS, STATE_P, GROUP_CH), (2 * GROUP_CH) ** -0.5),
        's5_c_re': nrm((N_SSM_LAYERS, 2, N_GROUPS, GROUP_CH, STATE_P), STATE_P ** -0.5),
        's5_c_im': nrm((N_SSM_LAYERS, 2, N_GROUPS, GROUP_CH, STATE_P), STATE_P ** -0.5),
        's5_d': nrm((N_SSM_LAYERS, D), 1.0),
        's5_w_glu': nrm((N_SSM_LAYERS, D, 2 * D), D ** -0.5),
        'attn_w_qkv': nrm((N_ATTN_LAYERS, D, qkv_dim), D ** -0.5),
        'attn_b_qkv': nrm((N_ATTN_LAYERS, qkv_dim), 0.01),
        'attn_w_o': nrm((N_ATTN_LAYERS, N_HEADS * HEAD_DIM, D), (N_HEADS * HEAD_DIM) ** -0.5),
        'attn_sink': nrm((N_ATTN_LAYERS, N_HEADS), 0.5),
        'moe_w_router': nrm((DEPTH, D, N_EXPERTS), D ** -0.5),
        'moe_router_bias': nrm((DEPTH, N_EXPERTS), 0.01),
        'moe_w_gate': nrm((DEPTH, N_EXPERTS, D, D_EXPERT), D ** -0.5),
        'moe_w_up': nrm((DEPTH, N_EXPERTS, D, D_EXPERT), D ** -0.5),
        'moe_w_down': nrm((DEPTH, N_EXPERTS, D_EXPERT, D), D_EXPERT ** -0.5),
        'moe_shared_gate': nrm((DEPTH, D, D_EXPERT), D ** -0.5),
        'moe_shared_up': nrm((DEPTH, D, D_EXPERT), D ** -0.5),
        'moe_shared_down': nrm((DEPTH, D_EXPERT, D), D_EXPERT ** -0.5),
    }


def reference(x_prompt, x_sample, c, state_ssm_re, state_ssm_im, cache_k, cache_v, c_ctx,
              ada_w, ada_b, norm1_g, norm2_g, final_norm_g,
              s5_lam_re, s5_lam_im, s5_log_step, s5_b_re, s5_b_im, s5_c_re, s5_c_im, s5_d, s5_w_glu,
              attn_w_qkv, attn_b_qkv, attn_w_o, attn_sink,
              moe_w_router, moe_router_bias, moe_w_gate, moe_w_up, moe_w_down,
              moe_shared_gate, moe_shared_up, moe_shared_down):
    xp, xs = x_prompt, x_sample
    zero_state = jnp.zeros((xp.shape[0], 2, N_GROUPS, STATE_P), F32)
    new_re, new_im, new_k, new_v = [], [], [], []
    for i in range(DEPTH):
        j = i // N_MIXERS
        sh1p, sc1p, g1p, sh2p, sc2p, g2p = _ada_params(c_ctx[None, :], ada_w[i], ada_b[i])
        sh1s, sc1s, g1s, sh2s, sc2s, g2s = _ada_params(c, ada_w[i], ada_b[i])
        hp = _rms_norm(xp, norm1_g[i]) * (1 + sc1p) + sh1p
        hs = _rms_norm(xs, norm1_g[i]) * (1 + sc1s) + sh1s
        if i % N_MIXERS == 0:
            s5 = (s5_lam_re[j], s5_lam_im[j], s5_log_step[j], s5_b_re[j], s5_b_im[j],
                  s5_c_re[j], s5_c_im[j], s5_d[j], s5_w_glu[j])
            op, st_re, st_im = _s5_mixer(hp, zero_state, zero_state, *s5)
            os_, _, _ = _s5_mixer(hs, state_ssm_re[:, j], state_ssm_im[:, j], *s5)
            new_re.append(st_re)
            new_im.append(st_im)
        else:
            op, k_ctx, v_ctx = _attn_context(hp, attn_w_qkv[j], attn_b_qkv[j], attn_w_o[j], attn_sink[j])
            os_ = _attn_latent(hs, cache_k[:, j], cache_v[:, j], attn_w_qkv[j], attn_b_qkv[j],
                               attn_w_o[j], attn_sink[j])
            new_k.append(k_ctx)
            new_v.append(v_ctx)
        xp = xp + g1p * op
        xs = xs + g1s * os_
        moe = (moe_w_router[i], moe_router_bias[i], moe_w_gate[i], moe_w_up[i], moe_w_down[i],
               moe_shared_gate[i], moe_shared_up[i], moe_shared_down[i])
        xp = xp + g2p * _moe(_rms_norm(xp, norm2_g[i]) * (1 + sc2p) + sh2p, *moe)
        xs = xs + g2s * _moe(_rms_norm(xs, norm2_g[i]) * (1 + sc2s) + sh2s, *moe)
    y_prompt = _rms_norm(xp, final_norm_g)
    y_sample = _rms_norm(xs, final_norm_g)
    return (y_prompt, y_sample, jnp.stack(new_re, axis=1), jnp.stack(new_im, axis=1),
            jnp.stack(new_k, axis=1), jnp.stack(new_v, axis=1))
```

```python
import functools
import math

import jax
import jax.numpy as jnp
from jax import lax
from jax.experimental import pallas as pl
from jax.experimental.pallas import tpu as pltpu

F32 = jnp.float32
BF16 = jnp.bfloat16
I32 = jnp.int32

GROUP_CH = 16
HEAD_DIM = 64
WINDOW = 128
ATTN_BLOCK = 128
GRID_W = 64
ROPE_BASE = 10000.0
TOP_K = 8
N_EXPERT_GROUPS = 8
TOPK_GROUPS = 4
ROUTED_SCALE = 2.5
RMS_EPS = 1e-6
NEG_INF = -1e30
N_MIXERS = 2

LANES = 128
SUBLANES = 8
VMEM_LIMIT_BYTES = 56 * 1024 * 1024

S5_CHUNK = 16
ROW_TILE = 256
EXPERT_TILE = 256
COMBINE_TILE = 128

_NT = (((1,), (1,)), ((), ()))


def _cparams(*sem):
    return pltpu.CompilerParams(dimension_semantics=sem, vmem_limit_bytes=VMEM_LIMIT_BYTES)


def _dot(a, b):
    return jnp.dot(a, b, preferred_element_type=F32)


def _split(x):
    hi = x.astype(BF16)
    lo = (x - hi.astype(F32)).astype(BF16)
    return hi, lo


def _dot3(a, b):
    a_hi, a_lo = _split(a)
    b_hi, b_lo = _split(b)
    return _dot(a_hi, b_hi) + _dot(a_lo, b_hi) + _dot(a_hi, b_lo)


def _sigmoid(x):
    return 1.0 / (1.0 + jnp.exp(-x))


def _gelu_tanh(x):
    c = math.sqrt(2.0 / math.pi)
    return 0.5 * x * (1.0 + jnp.tanh(c * (x + 0.044715 * (x * x * x))))


def _mod_index(tile, n_prompt_tiles, tiles_per_latent_seq):
    return jnp.where(tile < n_prompt_tiles, 0, 1 + (tile - n_prompt_tiles) // tiles_per_latent_seq)


def _ada_kernel(c_ref, w_ref, b_ref, o_ref):
    c = c_ref[...]
    o_ref[0] = _dot3(c * _sigmoid(c), w_ref[0]) + b_ref[0]


def _ada_params(cvec, ada_w, ada_b):
    depth, d, n = ada_w.shape
    rows = cvec.shape[0]
    tn = min(512, n)
    return pl.pallas_call(
        _ada_kernel,
        grid=(depth, n // tn),
        in_specs=[
            pl.BlockSpec((rows, d), lambda l, j: (0, 0)),
            pl.BlockSpec((1, d, tn), lambda l, j: (l, 0, j)),
            pl.BlockSpec((1, 1, tn), lambda l, j: (l, 0, j)),
        ],
        out_specs=pl.BlockSpec((1, rows, tn), lambda l, j: (l, 0, j)),
        out_shape=jax.ShapeDtypeStruct((depth, rows, n), F32),
        compiler_params=_cparams("arbitrary", "arbitrary"),
        name="ada_params",
    )(cvec, ada_w, ada_b.reshape(depth, 1, n))


def _norm_mod(x, g, sc, sh):
    ms = jnp.mean(x * x, axis=-1, keepdims=True)
    return (x * lax.rsqrt(ms + RMS_EPS) * g) * (1.0 + sc) + sh


def _norm_mod_kernel(x_ref, g_ref, sc_ref, sh_ref, o_ref):
    o_ref[...] = _norm_mod(x_ref[...], g_ref[...], sc_ref[0], sh_ref[0]).astype(o_ref.dtype)


def _norm_mod_call(x, g, sc, sh, geom, out_dtype):
    m, d = x.shape
    tm = ROW_TILE
    midx = functools.partial(_mod_index, n_prompt_tiles=geom["mp"] // tm,
                             tiles_per_latent_seq=geom["ls"] // tm)
    return pl.pallas_call(
        _norm_mod_kernel,
        grid=(m // tm,),
        in_specs=[
            pl.BlockSpec((tm, d), lambda i: (i, 0)),
            pl.BlockSpec((1, d), lambda i: (0, 0)),
            pl.BlockSpec((1, 1, d), lambda i: (midx(i), 0, 0)),
            pl.BlockSpec((1, 1, d), lambda i: (midx(i), 0, 0)),
        ],
        out_specs=pl.BlockSpec((tm, d), lambda i: (i, 0)),
        out_shape=jax.ShapeDtypeStruct((m, d), out_dtype),
        compiler_params=_cparams("arbitrary"),
        name="norm_mod",
    )(x, g.reshape(1, d), sc, sh)


def _mm_bias_kernel(x_ref, w_ref, b_ref, o_ref):
    o_ref[...] = _dot(x_ref[...], w_ref[...]) + b_ref[...]


def _mm_bias(x, w, b):
    m, k = x.shape
    n = w.shape[1]
    tm = min(512, m)
    tn = min(512, n)
    return pl.pallas_call(
        _mm_bias_kernel,
        grid=(n // tn, m // tm),
        in_specs=[
            pl.BlockSpec((tm, k), lambda j, i: (i, 0)),
            pl.BlockSpec((k, tn), lambda j, i: (0, j)),
            pl.BlockSpec((1, tn), lambda j, i: (0, j)),
        ],
        out_specs=pl.BlockSpec((tm, tn), lambda j, i: (i, j)),
        out_shape=jax.ShapeDtypeStruct((m, n), F32),
        compiler_params=_cparams("arbitrary", "arbitrary"),
        name="mm_bias",
    )(x, w, b.reshape(1, n))


def _mm_resid_kernel(a_ref, w_ref, x_ref, g_ref, o_ref):
    o_ref[...] = x_ref[...] + g_ref[0] * _dot(a_ref[...], w_ref[...])


def _mm_glu_resid_kernel(a_ref, wv_ref, wg_ref, x_ref, g_ref, o_ref):
    a = a_ref[...]
    val = _dot(a, wv_ref[...])
    gate = _dot(a, wg_ref[...])
    o_ref[...] = x_ref[...] + g_ref[0] * (val * _sigmoid(gate))


def _mm_resid(a, w, x, gate, geom, glu):
    m, k = a.shape
    n = x.shape[1]
    tm = ROW_TILE
    tn = min(1024, n)
    nj = n // tn
    midx = functools.partial(_mod_index, n_prompt_tiles=geom["mp"] // tm,
                             tiles_per_latent_seq=geom["ls"] // tm)
    a_spec = pl.BlockSpec((tm, k), lambda j, i: (i, 0))
    w_spec = pl.BlockSpec((k, tn), lambda j, i: (0, j))
    x_spec = pl.BlockSpec((tm, tn), lambda j, i: (i, j))
    g_spec = pl.BlockSpec((1, 1, tn), lambda j, i: (midx(i), 0, j))
    if glu:
        body = _mm_glu_resid_kernel
        in_specs = [a_spec, w_spec, pl.BlockSpec((k, tn), lambda j, i: (0, j + nj)), x_spec, g_spec]
        args = (a, w, w, x, gate)
    else:
        body = _mm_resid_kernel
        in_specs = [a_spec, w_spec, x_spec, g_spec]
        args = (a, w, x, gate)
    return pl.pallas_call(
        body,
        grid=(nj, m // tm),
        in_specs=in_specs,
        out_specs=pl.BlockSpec((tm, tn), lambda j, i: (i, j)),
        out_shape=jax.ShapeDtypeStruct((m, n), F32),
        compiler_params=_cparams("arbitrary", "arbitrary"),
        name="mm_glu_resid" if glu else "mm_resid",
    )(*args)


def _s5_operands(lam_re, lam_im, log_step, b_re, b_im, c_re, c_im, d_skip):
    t = S5_CHUNK
    n_groups, n_state = lam_re.shape[1:]
    ch = GROUP_CH
    tc = t * ch
    lam = lax.complex(jnp.minimum(lam_re.astype(F32), -1e-4), lam_im.astype(F32))
    step = jnp.exp(log_step.astype(F32))[..., None]
    lam_bar = jnp.exp(lam * step)
    b_bar = ((lam_bar - 1.0) / lam)[..., None] * lax.complex(b_re.astype(F32), b_im.astype(F32))
    c_mat = lax.complex(c_re.astype(F32), c_im.astype(F32))
    pw = [jnp.ones_like(lam_bar)]
    for _ in range(t):
        pw.append(pw[-1] * lam_bar)
    pw = jnp.stack(pw, axis=1)

    w1_f = jnp.einsum("tgp,gph->gthp", pw[0, t - 1::-1][:t], b_bar[0]).reshape(n_groups, tc, n_state)
    w1_b = jnp.einsum("tgp,gph->gthp", pw[1, :t], b_bar[1]).reshape(n_groups, tc, n_state)
    zeros = jnp.zeros((n_groups, tc, n_state), F32)

    def pair_cols(m, gi):
        return jnp.concatenate([m, zeros] if gi == 0 else [zeros, m], axis=-1)

    def w1_for(gi):
        sel = slice(gi, None, 2)
        return jnp.concatenate([
            pair_cols(jnp.real(w1_f), gi)[sel], pair_cols(jnp.imag(w1_f), gi)[sel],
            pair_cols(jnp.real(w1_b), gi)[sel], pair_cols(jnp.imag(w1_b), gi)[sel]], axis=-1)

    w1 = jnp.stack([w1_for(0), w1_for(1)], axis=1)

    k_f = jnp.real(jnp.einsum("gop,kgp,gpi->gkio", c_mat[0], pw[0, :t], b_bar[0]))
    k_b = jnp.real(jnp.einsum("gop,kgp,gpi->gkio", c_mat[1], pw[1, :t], b_bar[1]))
    s_idx = jnp.arange(t)[:, None]
    t_idx = jnp.arange(t)[None, :]
    lag_f = t_idx - s_idx
    lag_b = s_idx - t_idx
    toep_f = jnp.where((lag_f >= 0)[None, :, :, None, None], k_f[:, jnp.clip(lag_f, 0, t - 1)], 0.0)
    toep_b = jnp.where((lag_b >= 0)[None, :, :, None, None], k_b[:, jnp.clip(lag_b, 0, t - 1)], 0.0)
    eye_t = jnp.eye(t, dtype=F32)[None, :, :, None, None]
    skip = eye_t * (d_skip.astype(F32).reshape(n_groups, ch)[:, None, None, :, None]
                    * jnp.eye(ch, dtype=F32)[None, None, None])
    toep = (toep_f + toep_b + skip).transpose(0, 1, 3, 2, 4).reshape(n_groups, tc, tc)
    cf = jnp.einsum("gop,tgp->gpto", c_mat[0], pw[0, 1:t + 1]).reshape(n_groups, n_state, tc)
    cb = jnp.einsum("gop,tgp->gpto", c_mat[1], pw[1, t:0:-1]).reshape(n_groups, n_state, tc)
    zrow = jnp.zeros((n_groups, n_state, tc), F32)

    def pair_rows(m):
        even = jnp.concatenate([m, zrow], axis=1)
        odd = jnp.concatenate([zrow, m], axis=1)
        return jnp.where((jnp.arange(n_groups) % 2 == 0)[:, None, None], even, odd)

    w3 = jnp.concatenate([toep, pair_rows(jnp.real(cf)), pair_rows(-jnp.imag(cf)),
                          pair_rows(jnp.real(cb)), pair_rows(-jnp.imag(cb))], axis=1).astype(BF16)

    a_pow = pw[:, t]
    a_vec = jnp.stack([jnp.real(a_pow[0]), jnp.imag(a_pow[0]), jnp.real(a_pow[1]), jnp.imag(a_pow[1])],
                      axis=1).reshape(n_groups // 2, 2, 4, n_state)
    a_vec = a_vec.transpose(0, 2, 1, 3).reshape(n_groups // 2, 4, 2 * n_state)
    return w1, w3, a_vec


def _complex_step(a_re, a_im, h_re, h_im, s_re, s_im):
    return a_re * h_re - a_im * h_im + s_re, a_re * h_im + a_im * h_re + s_im


def _s5_scan(s_scr, h_scr, a, init, n_chunks, rows_per_chunk):
    pl_ = LANES
    if rows_per_chunk % SUBLANES == 0:
        rt = rows_per_chunk

        def body(c, carry):
            f_re, f_im, g_re, g_im = carry
            rf = pl.ds(pl.multiple_of(c * rt, SUBLANES), rt)
            rb = pl.ds(pl.multiple_of((n_chunks - 1 - c) * rt, SUBLANES), rt)
            h_scr[0, rf, :] = f_re
            h_scr[1, rf, :] = f_im
            h_scr[2, rb, :] = g_re
            h_scr[3, rb, :] = g_im
            f_re, f_im = _complex_step(a[0:1], a[1:2], f_re, f_im,
                                       s_scr[rf, 0:pl_], s_scr[rf, pl_:2 * pl_])
            g_re, g_im = _complex_step(a[2:3], a[3:4], g_re, g_im,
                                       s_scr[rb, 2 * pl_:3 * pl_], s_scr[rb, 3 * pl_:4 * pl_])
            return f_re, f_im, g_re, g_im

        return lax.fori_loop(0, n_chunks, body, tuple(init))

    assert rows_per_chunk * 2 == SUBLANES and n_chunks % 2 == 0
    half = rows_per_chunk
    n_tiles = n_chunks // 2
    low = lax.broadcasted_iota(I32, (SUBLANES, pl_), 0) < half

    def body(j, carry):
        f_re, f_im, g_re, g_im = carry
        rf = pl.ds(pl.multiple_of(j * SUBLANES, SUBLANES), SUBLANES)
        rb = pl.ds(pl.multiple_of((n_tiles - 1 - j) * SUBLANES, SUBLANES), SUBLANES)
        sf_re, sf_im = s_scr[rf, 0:pl_], s_scr[rf, pl_:2 * pl_]
        sb_re, sb_im = s_scr[rb, 2 * pl_:3 * pl_], s_scr[rb, 3 * pl_:4 * pl_]
        f1_re, f1_im = _complex_step(a[0:1], a[1:2], f_re, f_im, sf_re, sf_im)
        g1_re, g1_im = _complex_step(a[2:3], a[3:4], g_re, g_im, sb_re, sb_im)
        f1s_re, f1s_im = pltpu.roll(f1_re, half, 0), pltpu.roll(f1_im, half, 0)
        g1s_re, g1s_im = pltpu.roll(g1_re, half, 0), pltpu.roll(g1_im, half, 0)
        h_scr[0, rf, :] = jnp.where(low, f_re, f1s_re)
        h_scr[1, rf, :] = jnp.where(low, f_im, f1s_im)
        h_scr[2, rb, :] = jnp.where(low, g1s_re, g_re)
        h_scr[3, rb, :] = jnp.where(low, g1s_im, g_im)
        f2_re, f2_im = _complex_step(a[0:1], a[1:2], f1s_re, f1s_im, sf_re, sf_im)
        g2_re, g2_im = _complex_step(a[2:3], a[3:4], g1s_re, g1s_im, sb_re, sb_im)
        return (pltpu.roll(f2_re, half, 0), pltpu.roll(f2_im, half, 0),
                pltpu.roll(g2_re, half, 0), pltpu.roll(g2_im, half, 0))

    return lax.fori_loop(0, n_tiles, body, tuple(init))


def _s5_kernel(up_ref, us_ref, w1_ref, w3_ref, a_ref, h0_ref, gp_ref, gs_ref, fin_ref,
               sp_scr, ss_scr, hp_scr, hs_scr, *, geom):
    tc = up_ref.shape[-1]
    a = a_ref[0]
    w1 = [w1_ref[0, 0], w1_ref[0, 1]]
    w1_hi = [w.astype(BF16) for w in w1]

    sp_scr[...] = _dot3(up_ref[0], w1[0]) + _dot3(up_ref[1], w1[1])
    ss_scr[...] = _dot(us_ref[0].astype(BF16), w1_hi[0]) + _dot(us_ref[1].astype(BF16), w1_hi[1])

    zero = jnp.zeros((geom["bp"], LANES), F32)
    fin = _s5_scan(sp_scr, hp_scr, a, (zero, zero, zero, zero), geom["lp"] // S5_CHUNK, geom["bp"])
    for comp in range(4):
        fin_ref[0, comp] = fin[comp]
    _s5_scan(ss_scr, hs_scr, a, tuple(h0_ref[0, comp] for comp in range(4)),
             geom["ls"] // S5_CHUNK, geom["bs"])

    for u_ref, h_scr, o_ref in ((up_ref, hp_scr, gp_ref), (us_ref, hs_scr, gs_ref)):
        for gi in range(2):
            y = _dot(u_ref[gi].astype(BF16), w3_ref[gi, 0:tc, :])
            for comp in range(4):
                y += _dot(h_scr[comp].astype(BF16),
                          w3_ref[gi, tc + comp * LANES:tc + (comp + 1) * LANES, :])
            o_ref[gi] = _gelu_tanh(y).astype(o_ref.dtype)


def _s5_mixer(h, h0_re, h0_im, params, geom):
    m, d = h.shape
    t = S5_CHUNK
    ch = GROUP_CH
    n_groups = d // ch
    tc = t * ch
    bp, lp, bs, ls, mp = geom["bp"], geom["lp"], geom["bs"], geom["ls"], geom["mp"]
    rp, rs = mp // t, (m - mp) // t
    w1, w3, a_vec = _s5_operands(*params)
    n_state = a_vec.shape[-1] // 2

    def to_chunks(x, b, l):
        x = x.reshape(b, l // t, t, n_groups, ch).transpose(3, 1, 0, 2, 4)
        return x.reshape(n_groups, (l // t) * b, tc)

    def from_chunks(x, b, l):
        x = x.reshape(n_groups, l // t, b, t, ch).transpose(2, 1, 3, 0, 4)
        return x.reshape(b * l, d)

    up = to_chunks(h[:mp], bp, lp)
    us = to_chunks(h[mp:], bs, ls)
    h0 = jnp.stack([h0_re[:, 0], h0_im[:, 0], h0_re[:, 1], h0_im[:, 1]], axis=0).astype(F32)
    h0 = h0.reshape(4, bs, n_groups // 2, 2 * n_state).transpose(2, 0, 1, 3)
    reps = max(1, SUBLANES // bs)
    h0 = jnp.tile(h0, (1, 1, reps, 1))
    h0_rows = h0.shape[2]

    gp, gs, fin = pl.pallas_call(
        functools.partial(_s5_kernel, geom=geom),
        grid=(n_groups // 2,),
        in_specs=[
            pl.BlockSpec((2, rp, tc), lambda i: (i, 0, 0)),
            pl.BlockSpec((2, rs, tc), lambda i: (i, 0, 0)),
            pl.BlockSpec((1, 2, tc, 4 * LANES), lambda i: (i, 0, 0, 0)),
            pl.BlockSpec((2, tc + 4 * LANES, tc), lambda i: (i, 0, 0)),
            pl.BlockSpec((1, 4, LANES), lambda i: (i, 0, 0)),
            pl.BlockSpec((1, 4, h0_rows, LANES), lambda i: (i, 0, 0, 0)),
        ],
        out_specs=[
            pl.BlockSpec((2, rp, tc), lambda i: (i, 0, 0)),
            pl.BlockSpec((2, rs, tc), lambda i: (i, 0, 0)),
            pl.BlockSpec((1, 4, bp, LANES), lambda i: (i, 0, 0, 0)),
        ],
        out_shape=[
            jax.ShapeDtypeStruct((n_groups, rp, tc), BF16),
            jax.ShapeDtypeStruct((n_groups, rs, tc), BF16),
            jax.ShapeDtypeStruct((n_groups // 2, 4, bp, LANES), F32),
        ],
        scratch_shapes=[
            pltpu.VMEM((rp, 4 * LANES), F32),
            pltpu.VMEM((rs, 4 * LANES), F32),
            pltpu.VMEM((4, rp, LANES), F32),
            pltpu.VMEM((4, rs, LANES), F32),
        ],
        compiler_params=_cparams("arbitrary"),
        name="s5_mixer",
    )(up, us, w1, w3, a_vec, h0)

    g = jnp.concatenate([from_chunks(gp, bp, lp), from_chunks(gs, bs, ls)], axis=0)
    fin = fin.reshape(n_groups // 2, 2, 2, bp, 2, n_state).transpose(3, 1, 2, 0, 4, 5)
    fin = fin.reshape(bp, 2, 2, n_groups, n_state)
    return g, fin[:, :, 0], fin[:, :, 1]


def _softmax_with_sink(scores, sink):
    mx = sink
    for s in scores:
        mx = jnp.maximum(mx, jnp.max(s, axis=-1, keepdims=True))
    ps = [jnp.exp(s - mx) for s in scores]
    den = jnp.exp(sink - mx)
    for p in ps:
        den = den + jnp.sum(p, axis=-1, keepdims=True)
    return ps, 1.0 / den


def _attn_prompt_kernel(sink_ref, q_ref, kv_ref, o_ref, *, n_kv, q_per_kv):
    hd = HEAD_DIM
    scale = hd ** -0.5
    for kv in range(n_kv):
        k = kv_ref[:, kv * hd:(kv + 1) * hd].astype(BF16)
        v = kv_ref[:, (n_kv + kv) * hd:(n_kv + kv + 1) * hd].astype(BF16)
        for gq in range(q_per_kv):
            head = kv * q_per_kv + gq
            q = (q_ref[:, head * hd:(head + 1) * hd] * scale).astype(BF16)
            s = lax.dot_general(q, k, _NT, preferred_element_type=F32)
            (p,), inv = _softmax_with_sink([s], sink_ref[head])
            o_ref[:, head * hd:(head + 1) * hd] = (_dot(p.astype(BF16), v) * inv).astype(o_ref.dtype)


def _attn_prompt(qkv, sink, geom, n_kv):
    d = geom["d"]
    bp, lp = geom["bp"], geom["lp"]
    kvw = 2 * n_kv * HEAD_DIM
    n_heads = d // HEAD_DIM
    return pl.pallas_call(
        functools.partial(_attn_prompt_kernel, n_kv=n_kv, q_per_kv=n_heads // n_kv),
        grid=(bp,),
        in_specs=[
            pl.BlockSpec(memory_space=pltpu.SMEM),
            pl.BlockSpec((lp, d), lambda b: (b, 0)),
            pl.BlockSpec((lp, kvw), lambda b: (b, d // kvw)),
        ],
        out_specs=pl.BlockSpec((lp, d), lambda b: (b, 0)),
        out_shape=jax.ShapeDtypeStruct((bp * lp, d), BF16),
        compiler_params=_cparams("arbitrary"),
        name="attn_prompt",
    )(sink, qkv, qkv)


def _rope_tables(length):
    pairs = HEAD_DIM // 4
    pos = jnp.arange(length)
    row = (pos // GRID_W).astype(F32)
    col = (pos % GRID_W).astype(F32)
    inv_freq = ROPE_BASE ** (-jnp.arange(pairs, dtype=F32) / pairs)
    lane = jnp.arange(LANES) % HEAD_DIM
    axis_pos = jnp.where((lane // (2 * pairs))[None, :] == 0, row[:, None], col[:, None])
    ang = axis_pos * inv_freq[lane % pairs][None, :]
    sign = jnp.where((lane % (2 * pairs)) < pairs, -1.0, 1.0)[None, :]
    return jnp.cos(ang), jnp.sin(ang) * sign


def _rope(x, cos, sin):
    w = x.shape[1]
    pairs = HEAD_DIM // 4
    reps = w // LANES
    cos_w = jnp.concatenate([cos] * reps, axis=1) if reps > 1 else cos
    sin_w = jnp.concatenate([sin] * reps, axis=1) if reps > 1 else sin
    lane = lax.broadcasted_iota(I32, x.shape, 1)
    partner = jnp.where(lane % (2 * pairs) < pairs, pltpu.roll(x, w - pairs, 1), pltpu.roll(x, pairs, 1))
    return x * cos_w + partner * sin_w


def _attn_latent_kernel(sink_ref, q_ref, kvp_ref, kvo_ref, kvn_ref, ck_ref, cv_ref,
                        cos_p, sin_p, cos_o, sin_o, cos_n, sin_n, o_ref, *, n_kv, q_per_kv, n_blocks):
    hd = HEAD_DIM
    blk = ATTN_BLOCK
    scale = hd ** -0.5
    i = pl.program_id(1)
    kw = n_kv * hd
    k_win = jnp.concatenate([
        _rope(kvp_ref[:, 0:kw], cos_p[...], sin_p[...]),
        _rope(kvo_ref[:, 0:kw], cos_o[...], sin_o[...]),
        _rope(kvn_ref[:, 0:kw], cos_n[...], sin_n[...])], axis=0).astype(BF16)
    v_win = jnp.concatenate([kvp_ref[:, kw:2 * kw], kvo_ref[:, kw:2 * kw], kvn_ref[:, kw:2 * kw]],
                            axis=0).astype(BF16)
    rows = q_per_kv * blk
    r_in_blk = lax.broadcasted_iota(I32, (rows, 3 * blk), 0) % blk
    c_idx = lax.broadcasted_iota(I32, (rows, 3 * blk), 1)
    k_pos = (i - 1) * blk + c_idx
    mask = (jnp.abs(c_idx - blk - r_in_blk) <= WINDOW) & (k_pos >= 0) & (k_pos < n_blocks * blk)
    for kv in range(n_kv):
        qg = _rope(q_ref[:, kv * q_per_kv * hd:(kv + 1) * q_per_kv * hd], cos_o[...], sin_o[...]) * scale
        q = jnp.concatenate([qg[:, gq * hd:(gq + 1) * hd] for gq in range(q_per_kv)], axis=0).astype(BF16)
        sink = jnp.concatenate(
            [jnp.full((blk, 1), sink_ref[kv * q_per_kv + gq], F32) for gq in range(q_per_kv)], axis=0)
        s_ctx = lax.dot_general(q, ck_ref[0, :, kv * hd:(kv + 1) * hd].astype(BF16), _NT,
                                preferred_element_type=F32)
        s_win = lax.dot_general(q, k_win[:, kv * hd:(kv + 1) * hd], _NT, preferred_element_type=F32)
        s_win = jnp.where(mask, s_win, NEG_INF)
        (p_ctx, p_win), inv = _softmax_with_sink([s_ctx, s_win], sink)
        o = (_dot(p_ctx.astype(BF16), cv_ref[0, :, kv * hd:(kv + 1) * hd].astype(BF16))
             + _dot(p_win.astype(BF16), v_win[:, kv * hd:(kv + 1) * hd])) * inv
        for gq in range(q_per_kv):
            head = kv * q_per_kv + gq
            o_ref[:, head * hd:(head + 1) * hd] = o[gq * blk:(gq + 1) * blk].astype(o_ref.dtype)


def _attn_latent(qkv, ctx_k, ctx_v, sink, geom, n_kv):
    d = geom["d"]
    bs, ls, mp = geom["bs"], geom["ls"], geom["mp"]
    blk = ATTN_BLOCK
    nb = ls // blk
    kvw = 2 * n_kv * HEAD_DIM
    n_heads = d // HEAD_DIM
    past = ctx_k.shape[1]
    ck = ctx_k.reshape(bs, past, n_kv * HEAD_DIM).astype(F32)
    cv = ctx_v.reshape(bs, past, n_kv * HEAD_DIM).astype(F32)
    cos, sin = _rope_tables(ls)
    base = mp // blk

    def prev(i):
        return jnp.maximum(i - 1, 0)

    def nxt(i):
        return jnp.minimum(i + 1, nb - 1)

    def kv_spec(f):
        return pl.BlockSpec((blk, kvw), lambda b, i: (base + b * nb + f(i), d // kvw))

    def tab_spec(f):
        return pl.BlockSpec((blk, LANES), lambda b, i: (f(i), 0))

    same = lambda i: i
    return pl.pallas_call(
        functools.partial(_attn_latent_kernel, n_kv=n_kv, q_per_kv=n_heads // n_kv, n_blocks=nb),
        grid=(bs, nb),
        in_specs=[
            pl.BlockSpec(memory_space=pltpu.SMEM),
            pl.BlockSpec((blk, d), lambda b, i: (base + b * nb + i, 0)),
            kv_spec(prev), kv_spec(same), kv_spec(nxt),
            pl.BlockSpec((1, past, n_kv * HEAD_DIM), lambda b, i: (b, 0, 0)),
            pl.BlockSpec((1, past, n_kv * HEAD_DIM), lambda b, i: (b, 0, 0)),
            tab_spec(prev), tab_spec(prev), tab_spec(same), tab_spec(same), tab_spec(nxt), tab_spec(nxt),
        ],
        out_specs=pl.BlockSpec((blk, d), lambda b, i: (b * nb + i, 0)),
        out_shape=jax.ShapeDtypeStruct((bs * ls, d), BF16),
        compiler_params=_cparams("arbitrary", "arbitrary"),
        name="attn_latent",
    )(sink, qkv, qkv, qkv, qkv, ck, cv, cos, sin, cos, sin, cos, sin)


def _router_kernel(x_ref, g_ref, sc_ref, sh_ref, wt_hi_ref, wt_lo_ref, bias_ref,
                   h_ref, eidx_ref, w_ref, rank_ref, cnt_ref, base_scr, *, n_experts):
    step = pl.program_id(0)

    @pl.when(step == 0)
    def _():
        base_scr[...] = jnp.zeros_like(base_scr)

    h = _norm_mod(x_ref[...], g_ref[...], sc_ref[0], sh_ref[0])
    h_ref[...] = h
    tm = h.shape[0]
    h_hi, h_lo = _split(h)
    wt_hi = wt_hi_ref[...]
    logits = (lax.dot_general(wt_hi, h_hi, _NT, preferred_element_type=F32)
              + lax.dot_general(wt_lo_ref[...], h_hi, _NT, preferred_element_type=F32)
              + lax.dot_general(wt_hi, h_lo, _NT, preferred_element_type=F32))
    scores = _sigmoid(logits)
    sel = scores + bias_ref[:, 0:1]
    per_group = n_experts // N_EXPERT_GROUPS
    e_iota = lax.broadcasted_iota(I32, (n_experts, tm), 0)
    big = jnp.int32(1 << 30)

    grp = []
    for gidx in range(N_EXPERT_GROUPS):
        v = sel[gidx * per_group:(gidx + 1) * per_group]
        r = lax.broadcasted_iota(I32, v.shape, 0) + gidx * per_group
        m1 = jnp.max(v, axis=0, keepdims=True)
        i1 = jnp.min(jnp.where(v == m1, r, big), axis=0, keepdims=True)
        m2 = jnp.max(jnp.where(r == i1, -jnp.inf, v), axis=0, keepdims=True)
        grp.append(jnp.broadcast_to(m1 + m2, v.shape))
    cur = jnp.concatenate(grp, axis=0)
    g_iota = e_iota // per_group
    cand = jnp.full(sel.shape, -jnp.inf, F32)
    for _ in range(TOPK_GROUPS):
        mx = jnp.max(cur, axis=0, keepdims=True)
        gi = jnp.min(jnp.where(cur == mx, g_iota, big), axis=0, keepdims=True)
        hit = g_iota == gi
        cand = jnp.where(hit, sel, cand)
        cur = jnp.where(hit, -jnp.inf, cur)

    idxs, wts = [], []
    onehot = jnp.zeros((n_experts, tm), F32)
    for _ in range(TOP_K):
        mx = jnp.max(cand, axis=0, keepdims=True)
        ei = jnp.min(jnp.where(cand == mx, e_iota, big), axis=0, keepdims=True)
        hit = e_iota == ei
        idxs.append(ei)
        wts.append(jnp.sum(jnp.where(hit, scores, 0.0), axis=0, keepdims=True))
        onehot = jnp.where(hit, 1.0, onehot)
        cand = jnp.where(hit, -jnp.inf, cand)
    wsum = wts[0]
    for wk in wts[1:]:
        wsum = wsum + wk
    eidx_ref[...] = jnp.concatenate(idxs, axis=0)
    w_ref[...] = jnp.concatenate(wts, axis=0) / wsum * ROUTED_SCALE

    t_row = lax.broadcasted_iota(I32, (tm, tm), 0)
    t_col = lax.broadcasted_iota(I32, (tm, tm), 1)
    before = jnp.where(t_row < t_col, 1.0, 0.0).astype(BF16)
    base = base_scr[:, 0:1]
    pos = base + _dot(onehot.astype(BF16), before)
    ranks = [jnp.sum(jnp.where(e_iota == ei, pos, 0.0), axis=0, keepdims=True) for ei in idxs]
    rank_ref[...] = jnp.concatenate(ranks, axis=0).astype(I32)
    base_scr[...] = base_scr[...] + jnp.sum(onehot, axis=1, keepdims=True)
    cnt_ref[...] = base_scr[...]


def _route(x, g, sc, sh, w_router, bias, geom):
    m, d = x.shape
    n_experts = w_router.shape[1]
    tm = ROW_TILE
    wt = w_router.astype(F32).T
    wt_hi = wt.astype(BF16)
    wt_lo = (wt - wt_hi.astype(F32)).astype(BF16)
    bias_b = jnp.broadcast_to(bias.astype(F32)[:, None], (n_experts, LANES))
    midx = functools.partial(_mod_index, n_prompt_tiles=geom["mp"] // tm,
                             tiles_per_latent_seq=geom["ls"] // tm)
    tok_spec = pl.BlockSpec((TOP_K, tm), lambda i: (0, i))
    return pl.pallas_call(
        functools.partial(_router_kernel, n_experts=n_experts),
        grid=(m // tm,),
        in_specs=[
            pl.BlockSpec((tm, d), lambda i: (i, 0)),
            pl.BlockSpec((1, d), lambda i: (0, 0)),
            pl.BlockSpec((1, 1, d), lambda i: (midx(i), 0, 0)),
            pl.BlockSpec((1, 1, d), lambda i: (midx(i), 0, 0)),
            pl.BlockSpec((n_experts, d), lambda i: (0, 0)),
            pl.BlockSpec((n_experts, d), lambda i: (0, 0)),
            pl.BlockSpec((n_experts, LANES), lambda i: (0, 0)),
        ],
        out_specs=[
            pl.BlockSpec((tm, d), lambda i: (i, 0)),
            tok_spec, tok_spec, tok_spec,
            pl.BlockSpec((n_experts, LANES), lambda i: (0, 0)),
        ],
        out_shape=[
            jax.ShapeDtypeStruct((m, d), F32),
            jax.ShapeDtypeStruct((TOP_K, m), I32),
            jax.ShapeDtypeStruct((TOP_K, m), F32),
            jax.ShapeDtypeStruct((TOP_K, m), I32),
            jax.ShapeDtypeStruct((n_experts, LANES), F32),
        ],
        scratch_shapes=[pltpu.VMEM((n_experts, LANES), F32)],
        compiler_params=_cparams("arbitrary"),
        name="moe_router",
    )(x, g.reshape(1, d), sc, sh, wt_hi, wt_lo, bias_b)


def _dispatch_kernel(fill_start_ref, fill_n_ref, slot_hbm, h_ref, xs_hbm, slot_smem, zero_scr,
                     idx_sem, row_sem, *, n_experts):
    step = pl.program_id(0)
    tm = h_ref.shape[0]
    n_assign = TOP_K * tm
    idx_copy = pltpu.make_async_copy(slot_hbm.at[step], slot_smem, idx_sem)
    idx_copy.start()
    idx_copy.wait()

    def row_copy(t, slot):
        return pltpu.make_async_copy(h_ref.at[pl.ds(t, 1)], xs_hbm.at[pl.ds(slot, 1)], row_sem)

    def issue(j, carry):
        row_copy(j % tm, slot_smem[j]).start()
        return carry

    lax.fori_loop(0, n_assign, issue, 0)

    def drain(j, carry):
        row_copy(0, 0).wait()
        return carry

    lax.fori_loop(0, n_assign, drain, 0)

    @pl.when(step == pl.num_programs(0) - 1)
    def _():
        zero_scr[...] = jnp.zeros_like(zero_scr)

        def pad_copy(slot):
            return pltpu.make_async_copy(zero_scr.at[pl.ds(0, 1)], xs_hbm.at[pl.ds(slot, 1)], row_sem)

        def per_expert(e, total):
            start = fill_start_ref[e]
            n = fill_n_ref[e]

            def one(r, carry):
                pad_copy(start + r).start()
                return carry

            lax.fori_loop(0, n, one, 0)
            return total + n

        total = lax.fori_loop(0, n_experts, per_expert, jnp.int32(0))

        def drain_pad(j, carry):
            pad_copy(0).wait()
            return carry

        lax.fori_loop(0, total, drain_pad, 0)


def _dispatch(h, slot, fill_start, fill_n, n_slots):
    m, d = h.shape
    tm = ROW_TILE
    n_tiles = m // tm
    n_experts = fill_start.shape[0]
    slot_tiles = slot.reshape(TOP_K, n_tiles, tm).transpose(1, 0, 2).reshape(n_tiles, TOP_K * tm)
    return pl.pallas_call(
        functools.partial(_dispatch_kernel, n_experts=n_experts),
        grid_spec=pltpu.PrefetchScalarGridSpec(
            num_scalar_prefetch=2,
            grid=(n_tiles,),
            in_specs=[
                pl.BlockSpec(memory_space=pl.ANY),
                pl.BlockSpec((tm, d), lambda i, fs, fn: (i, 0)),
            ],
            out_specs=pl.BlockSpec(memory_space=pl.ANY),
            scratch_shapes=[
                pltpu.SMEM((TOP_K * tm,), I32),
                pltpu.VMEM((SUBLANES, d), F32),
                pltpu.SemaphoreType.DMA,
                pltpu.SemaphoreType.DMA,
            ],
        ),
        out_shape=jax.ShapeDtypeStruct((n_slots, d), F32),
        compiler_params=_cparams("arbitrary"),
        name="moe_dispatch",
    )(fill_start, fill_n, slot_tiles, h)


def _expert_kernel(be_ref, na_ref, xs_ref, wg_ref, wu_ref, wd_ref, y_ref, wg_scr, wu_scr, wd_scr):
    i = pl.program_id(0)

    @pl.when(i < na_ref[0])
    def _():
        first = jnp.logical_or(i == 0, be_ref[i] != be_ref[jnp.maximum(i - 1, 0)])

        @pl.when(first)
        def _():
            wg_scr[...] = wg_ref[...].astype(BF16)
            wu_scr[...] = wu_ref[...].astype(BF16)
            wd_scr[...] = wd_ref[...].astype(BF16)

        x = xs_ref[...].astype(BF16)
        gate = _dot(x, wg_scr[...])
        up = _dot(x, wu_scr[...])
        act = (gate * _sigmoid(gate) * up).astype(BF16)
        y_ref[...] = _dot(act, wd_scr[...])


def _experts(xs, block_expert, n_active, w_gate, w_up, w_down, layer):
    n_slots, d = xs.shape
    tm = EXPERT_TILE
    de = w_gate.shape[-1]
    n_blocks = n_slots // tm

    def blk(i, be, na):
        return jnp.minimum(i, na[0] - 1)

    return pl.pallas_call(
        _expert_kernel,
        grid_spec=pltpu.PrefetchScalarGridSpec(
            num_scalar_prefetch=2,
            grid=(n_blocks,),
            in_specs=[
                pl.BlockSpec((tm, d), lambda i, be, na: (blk(i, be, na), 0)),
                pl.BlockSpec((None, None, d, de), lambda i, be, na: (layer, be[blk(i, be, na)], 0, 0)),
                pl.BlockSpec((None, None, d, de), lambda i, be, na: (layer, be[blk(i, be, na)], 0, 0)),
                pl.BlockSpec((None, None, de, d), lambda i, be, na: (layer, be[blk(i, be, na)], 0, 0)),
            ],
            out_specs=pl.BlockSpec((tm, d), lambda i, be, na: (blk(i, be, na), 0)),
            scratch_shapes=[
                pltpu.VMEM((d, de), BF16),
                pltpu.VMEM((d, de), BF16),
                pltpu.VMEM((de, d), BF16),
            ],
        ),
        out_shape=jax.ShapeDtypeStruct((n_slots, d), F32),
        compiler_params=_cparams("arbitrary"),
        name="moe_experts",
    )(block_expert, n_active, xs, w_gate, w_up, w_down)


def _combine_kernel(slot_hbm, y_hbm, x_ref, h_ref, w_ref, g_ref, sg_ref, su_ref, sd_ref, o_ref,
                    slot_smem, buf, idx_sem, row_sem):
    step = pl.program_id(0)
    tm = x_ref.shape[0]
    n_assign = TOP_K * tm
    idx_copy = pltpu.make_async_copy(slot_hbm.at[step], slot_smem, idx_sem)
    idx_copy.start()
    idx_copy.wait()

    def row_copy(j, slot):
        return pltpu.make_async_copy(y_hbm.at[pl.ds(slot, 1)], buf.at[pl.ds(j, 1)], row_sem)

    def issue(j, carry):
        row_copy(j, slot_smem[j]).start()
        return carry

    lax.fori_loop(0, n_assign, issue, 0)

    hb = h_ref[...].astype(BF16)
    gate = _dot(hb, sg_ref[...])
    up = _dot(hb, su_ref[...])
    acc = _dot((gate * _sigmoid(gate) * up).astype(BF16), sd_ref[...])

    def drain(j, carry):
        row_copy(0, 0).wait()
        return carry

    lax.fori_loop(0, n_assign, drain, 0)

    w = w_ref[...]
    for k in range(TOP_K):
        acc = acc + w[:, k:k + 1] * buf[k * tm:(k + 1) * tm, :]
    o_ref[...] = x_ref[...] + g_ref[0] * acc


def _combine(slot, y, x, h, w_tok, gate, s_gate, s_up, s_down, geom, layer):
    m, d = x.shape
    tm = COMBINE_TILE
    n_tiles = m // tm
    de = s_gate.shape[-1]
    slot_tiles = slot.reshape(TOP_K, n_tiles, tm).transpose(1, 0, 2).reshape(n_tiles, TOP_K * tm)
    midx = functools.partial(_mod_index, n_prompt_tiles=geom["mp"] // tm,
                             tiles_per_latent_seq=geom["ls"] // tm)
    return pl.pallas_call(
        _combine_kernel,
        grid=(n_tiles,),
        in_specs=[
            pl.BlockSpec(memory_space=pl.ANY),
            pl.BlockSpec(memory_space=pl.ANY),
            pl.BlockSpec((tm, d), lambda i: (i, 0)),
            pl.BlockSpec((tm, d), lambda i: (i, 0)),
            pl.BlockSpec((tm, TOP_K), lambda i: (i, 0)),
            pl.BlockSpec((1, 1, d), lambda i: (midx(i), 0, 0)),
            pl.BlockSpec((None, d, de), lambda i: (layer, 0, 0)),
            pl.BlockSpec((None, d, de), lambda i: (layer, 0, 0)),
            pl.BlockSpec((None, de, d), lambda i: (layer, 0, 0)),
        ],
        out_specs=pl.BlockSpec((tm, d), lambda i: (i, 0)),
        out_shape=jax.ShapeDtypeStruct((m, d), F32),
        scratch_shapes=[
            pltpu.SMEM((TOP_K * tm,), I32),
            pltpu.VMEM((TOP_K * tm, d), F32),
            pltpu.SemaphoreType.DMA,
            pltpu.SemaphoreType.DMA,
        ],
        compiler_params=_cparams("arbitrary"),
        name="moe_combine",
    )(slot_tiles, y, x, h, w_tok, gate, s_gate, s_up, s_down)


def _moe_layer(x, norm_g, sc, sh, gate, w_router, bias, w_gate, w_up, w_down, s_gate, s_up, s_down,
               geom, layer):
    m, d = x.shape
    n_experts = w_router.shape[1]
    tm = EXPERT_TILE
    h, eidx, w_sel, rank, counts = _route(x, norm_g, sc, sh, w_router, bias, geom)

    counts = counts[:, 0].astype(I32)
    padded = (counts + tm - 1) // tm * tm
    pad_end = jnp.cumsum(padded)
    pad_start = pad_end - padded
    slot = pad_start[eidx] + rank
    n_blocks = (m * TOP_K) // tm + n_experts
    n_active = (pad_end[-1] // tm).astype(I32).reshape(1)
    block_expert = jnp.minimum(
        jnp.searchsorted(pad_end, jnp.arange(n_blocks, dtype=I32) * tm, side="right"),
        n_experts - 1).astype(I32)

    xs = _dispatch(h, slot, pad_start + counts, padded - counts, n_blocks * tm)
    y = _experts(xs, block_expert, n_active, w_gate, w_up, w_down, layer)
    return _combine(slot, y, x, h, w_sel.T, gate, s_gate.astype(BF16), s_up.astype(BF16),
                    s_down.astype(BF16), geom, layer)


def kernel(x_prompt, x_sample, c, state_ssm_re, state_ssm_im, cache_k, cache_v, c_ctx, ada_w, ada_b, norm1_g, norm2_g, final_norm_g, s5_lam_re, s5_lam_im, s5_log_step, s5_b_re, s5_b_im, s5_c_re, s5_c_im, s5_d, s5_w_glu, attn_w_qkv, attn_b_qkv, attn_w_o, attn_sink, moe_w_router, moe_router_bias, moe_w_gate, moe_w_up, moe_w_down, moe_shared_gate, moe_shared_up, moe_shared_down):
    bp, lp, d = x_prompt.shape
    bs, ls, _ = x_sample.shape
    depth = ada_w.shape[0]
    n_kv = cache_k.shape[3]
    mp, ms = bp * lp, bs * ls
    geom = dict(bp=bp, lp=lp, bs=bs, ls=ls, mp=mp, d=d)
    assert lp % ROW_TILE == 0 and ls % ROW_TILE == 0 and bs in (4,) and bp % SUBLANES == 0
    assert (2 * n_kv * HEAD_DIM) <= d and d % (2 * n_kv * HEAD_DIM) == 0

    x = jnp.concatenate([x_prompt.reshape(mp, d), x_sample.reshape(ms, d)], axis=0).astype(F32)

    n_mod = -(-(bs + 1) // SUBLANES) * SUBLANES
    cvec = jnp.zeros((n_mod, d), F32).at[0].set(c_ctx.astype(F32)).at[1:bs + 1].set(c.astype(F32))
    mods = _ada_params(cvec, ada_w, ada_b).reshape(depth, n_mod, 6, 1, d)

    new_re, new_im, new_k, new_v = [], [], [], []
    for i in range(depth):
        j = i // N_MIXERS
        sh1, sc1, g1, sh2, sc2, g2 = (mods[i, :, part] for part in range(6))
        if i % N_MIXERS == 0:
            h = _norm_mod_call(x, norm1_g[i], sc1, sh1, geom, F32)
            params = (s5_lam_re[j], s5_lam_im[j], s5_log_step[j], s5_b_re[j], s5_b_im[j],
                      s5_c_re[j], s5_c_im[j], s5_d[j])
            g, st_re, st_im = _s5_mixer(h, state_ssm_re[:, j], state_ssm_im[:, j], params, geom)
            new_re.append(st_re)
            new_im.append(st_im)
            x = _mm_resid(g, s5_w_glu[j].astype(BF16), x, g1, geom, glu=True)
        else:
            h = _norm_mod_call(x, norm1_g[i], sc1, sh1, geom, BF16)
            qkv = _mm_bias(h, attn_w_qkv[j].astype(BF16), attn_b_qkv[j].astype(F32))
            sink = attn_sink[j].astype(F32)
            o = jnp.concatenate([
                _attn_prompt(qkv, sink, geom, n_kv),
                _attn_latent(qkv, cache_k[:, j], cache_v[:, j], sink, geom, n_kv)], axis=0)
            kw = n_kv * HEAD_DIM
            new_k.append(qkv[:mp, d:d + kw].reshape(bp, lp, n_kv, HEAD_DIM))
            new_v.append(qkv[:mp, d + kw:d + 2 * kw].reshape(bp, lp, n_kv, HEAD_DIM))
            x = _mm_resid(o, attn_w_o[j].astype(BF16), x, g1, geom, glu=False)
        x = _moe_layer(x, norm2_g[i], sc2, sh2, g2, moe_w_router[i], moe_router_bias[i],
                       moe_w_gate, moe_w_up, moe_w_down, moe_shared_gate, moe_shared_up, moe_shared_down,
                       geom, i)

    zeros = jnp.zeros((n_mod, 1, d), F32)
    y = _norm_mod_call(x, final_norm_g, zeros, zeros, geom, F32)
    return (y[:mp].reshape(bp, lp, d), y[mp:].reshape(bs, ls, d),
            jnp.stack(new_re, axis=1), jnp.stack(new_im, axis=1),
            jnp.stack(new_k, axis=1), jnp.stack(new_v, axis=1))
```

```python
import functools
import math

import jax
import jax.numpy as jnp
from jax import lax
from jax.experimental import pallas as pl
from jax.experimental.pallas import tpu as pltpu

F32 = jnp.float32
BF16 = jnp.bfloat16
I32 = jnp.int32

GROUP_CH = 16
HEAD_DIM = 64
WINDOW = 128
ATTN_BLOCK = 128
GRID_W = 64
ROPE_BASE = 10000.0
TOP_K = 8
N_EXPERT_GROUPS = 8
TOPK_GROUPS = 4
ROUTED_SCALE = 2.5
RMS_EPS = 1e-6
NEG_INF = -1e30
N_MIXERS = 2

LANES = 128
SUBLANES = 8
VMEM_LIMIT_BYTES = 56 * 1024 * 1024

S5_CHUNK = 16
ROW_TILE = 256
EXPERT_TILE = 256
MOE_TILE = 128
DMA_WINDOW = 8
U32 = jnp.uint32

_NT = (((1,), (1,)), ((), ()))


def _cparams(*sem):
    return pltpu.CompilerParams(dimension_semantics=sem, vmem_limit_bytes=VMEM_LIMIT_BYTES)


def _dot(a, b):
    return jnp.dot(a, b, preferred_element_type=F32)


def _split(x):
    hi = x.astype(BF16)
    lo = (x - hi.astype(F32)).astype(BF16)
    return hi, lo


def _dot3(a, b):
    a_hi, a_lo = _split(a)
    b_hi, b_lo = _split(b)
    return _dot(a_hi, b_hi) + _dot(a_lo, b_hi) + _dot(a_hi, b_lo)


def _sigmoid(x):
    return 1.0 / (1.0 + jnp.exp(-x))


def _gelu_tanh(x):
    c = math.sqrt(2.0 / math.pi)
    return 0.5 * x * (1.0 + jnp.tanh(c * (x + 0.044715 * (x * x * x))))


def _mod_index(tile, n_prompt_tiles, tiles_per_latent_seq):
    return jnp.where(tile < n_prompt_tiles, 0, 1 + (tile - n_prompt_tiles) // tiles_per_latent_seq)


def _ada_kernel(c_ref, w_ref, b_ref, o_ref):
    c = c_ref[...]
    o_ref[0] = _dot3(c * _sigmoid(c), w_ref[0]) + b_ref[0]


def _ada_params(cvec, ada_w, ada_b):
    depth, d, n = ada_w.shape
    rows = cvec.shape[0]
    tn = min(512, n)
    return pl.pallas_call(
        _ada_kernel,
        grid=(depth, n // tn),
        in_specs=[
            pl.BlockSpec((rows, d), lambda l, j: (0, 0)),
            pl.BlockSpec((1, d, tn), lambda l, j: (l, 0, j)),
            pl.BlockSpec((1, 1, tn), lambda l, j: (l, 0, j)),
        ],
        out_specs=pl.BlockSpec((1, rows, tn), lambda l, j: (l, 0, j)),
        out_shape=jax.ShapeDtypeStruct((depth, rows, n), F32),
        compiler_params=_cparams("arbitrary", "arbitrary"),
        name="ada_params",
    )(cvec, ada_w, ada_b.reshape(depth, 1, n))


def _norm_mod(x, g, sc, sh):
    ms = jnp.mean(x * x, axis=-1, keepdims=True)
    return (x * lax.rsqrt(ms + RMS_EPS) * g) * (1.0 + sc) + sh


def _norm_mod_kernel(x_ref, g_ref, sc_ref, sh_ref, o_ref):
    o_ref[...] = _norm_mod(x_ref[...], g_ref[...], sc_ref[0], sh_ref[0]).astype(o_ref.dtype)


def _norm_mod_call(x, g, sc, sh, geom, out_dtype):
    m, d = x.shape
    tm = ROW_TILE
    midx = functools.partial(_mod_index, n_prompt_tiles=geom["mp"] // tm,
                             tiles_per_latent_seq=geom["ls"] // tm)
    return pl.pallas_call(
        _norm_mod_kernel,
        grid=(m // tm,),
        in_specs=[
            pl.BlockSpec((tm, d), lambda i: (i, 0)),
            pl.BlockSpec((1, d), lambda i: (0, 0)),
            pl.BlockSpec((1, 1, d), lambda i: (midx(i), 0, 0)),
            pl.BlockSpec((1, 1, d), lambda i: (midx(i), 0, 0)),
        ],
        out_specs=pl.BlockSpec((tm, d), lambda i: (i, 0)),
        out_shape=jax.ShapeDtypeStruct((m, d), out_dtype),
        compiler_params=_cparams("arbitrary"),
        name="norm_mod",
    )(x, g.reshape(1, d), sc, sh)


def _mm_bias_kernel(x_ref, w_ref, b_ref, o_ref):
    o_ref[...] = _dot(x_ref[...], w_ref[...]) + b_ref[...]


def _mm_bias(x, w, b):
    m, k = x.shape
    n = w.shape[1]
    tm = min(512, m)
    tn = min(512, n)
    return pl.pallas_call(
        _mm_bias_kernel,
        grid=(n // tn, m // tm),
        in_specs=[
            pl.BlockSpec((tm, k), lambda j, i: (i, 0)),
            pl.BlockSpec((k, tn), lambda j, i: (0, j)),
            pl.BlockSpec((1, tn), lambda j, i: (0, j)),
        ],
        out_specs=pl.BlockSpec((tm, tn), lambda j, i: (i, j)),
        out_shape=jax.ShapeDtypeStruct((m, n), F32),
        compiler_params=_cparams("arbitrary", "arbitrary"),
        name="mm_bias",
    )(x, w, b.reshape(1, n))


def _mm_resid_kernel(a_ref, w_ref, x_ref, g_ref, o_ref):
    o_ref[...] = x_ref[...] + g_ref[0] * _dot(a_ref[...], w_ref[...])


def _mm_glu_resid_kernel(a_ref, wv_ref, wg_ref, x_ref, g_ref, o_ref):
    a = a_ref[...]
    val = _dot(a, wv_ref[...])
    gate = _dot(a, wg_ref[...])
    o_ref[...] = x_ref[...] + g_ref[0] * (val * _sigmoid(gate))


def _mm_resid(a, w, x, gate, geom, glu):
    m, k = a.shape
    n = x.shape[1]
    tm = ROW_TILE
    tn = min(1024, n)
    nj = n // tn
    midx = functools.partial(_mod_index, n_prompt_tiles=geom["mp"] // tm,
                             tiles_per_latent_seq=geom["ls"] // tm)
    a_spec = pl.BlockSpec((tm, k), lambda j, i: (i, 0))
    w_spec = pl.BlockSpec((k, tn), lambda j, i: (0, j))
    x_spec = pl.BlockSpec((tm, tn), lambda j, i: (i, j))
    g_spec = pl.BlockSpec((1, 1, tn), lambda j, i: (midx(i), 0, j))
    if glu:
        body = _mm_glu_resid_kernel
        in_specs = [a_spec, w_spec, pl.BlockSpec((k, tn), lambda j, i: (0, j + nj)), x_spec, g_spec]
        args = (a, w, w, x, gate)
    else:
        body = _mm_resid_kernel
        in_specs = [a_spec, w_spec, x_spec, g_spec]
        args = (a, w, x, gate)
    return pl.pallas_call(
        body,
        grid=(nj, m // tm),
        in_specs=in_specs,
        out_specs=pl.BlockSpec((tm, tn), lambda j, i: (i, j)),
        out_shape=jax.ShapeDtypeStruct((m, n), F32),
        compiler_params=_cparams("arbitrary", "arbitrary"),
        name="mm_glu_resid" if glu else "mm_resid",
    )(*args)


def _s5_operands(lam_re, lam_im, log_step, b_re, b_im, c_re, c_im, d_skip):
    t = S5_CHUNK
    n_groups, n_state = lam_re.shape[1:]
    ch = GROUP_CH
    tc = t * ch
    lam = lax.complex(jnp.minimum(lam_re.astype(F32), -1e-4), lam_im.astype(F32))
    step = jnp.exp(log_step.astype(F32))[..., None]
    lam_bar = jnp.exp(lam * step)
    b_bar = ((lam_bar - 1.0) / lam)[..., None] * lax.complex(b_re.astype(F32), b_im.astype(F32))
    c_mat = lax.complex(c_re.astype(F32), c_im.astype(F32))
    pw = [jnp.ones_like(lam_bar)]
    for _ in range(t):
        pw.append(pw[-1] * lam_bar)
    pw = jnp.stack(pw, axis=1)

    w1_f = jnp.einsum("tgp,gph->gthp", pw[0, t - 1::-1][:t], b_bar[0]).reshape(n_groups, tc, n_state)
    w1_b = jnp.einsum("tgp,gph->gthp", pw[1, :t], b_bar[1]).reshape(n_groups, tc, n_state)
    zeros = jnp.zeros((n_groups, tc, n_state), F32)

    def pair_cols(m, gi):
        return jnp.concatenate([m, zeros] if gi == 0 else [zeros, m], axis=-1)

    def w1_for(gi):
        sel = slice(gi, None, 2)
        return jnp.concatenate([
            pair_cols(jnp.real(w1_f), gi)[sel], pair_cols(jnp.imag(w1_f), gi)[sel],
            pair_cols(jnp.real(w1_b), gi)[sel], pair_cols(jnp.imag(w1_b), gi)[sel]], axis=-1)

    w1 = jnp.stack([w1_for(0), w1_for(1)], axis=1)

    k_f = jnp.real(jnp.einsum("gop,kgp,gpi->gkio", c_mat[0], pw[0, :t], b_bar[0]))
    k_b = jnp.real(jnp.einsum("gop,kgp,gpi->gkio", c_mat[1], pw[1, :t], b_bar[1]))
    s_idx = jnp.arange(t)[:, None]
    t_idx = jnp.arange(t)[None, :]
    lag_f = t_idx - s_idx
    lag_b = s_idx - t_idx
    toep_f = jnp.where((lag_f >= 0)[None, :, :, None, None], k_f[:, jnp.clip(lag_f, 0, t - 1)], 0.0)
    toep_b = jnp.where((lag_b >= 0)[None, :, :, None, None], k_b[:, jnp.clip(lag_b, 0, t - 1)], 0.0)
    eye_t = jnp.eye(t, dtype=F32)[None, :, :, None, None]
    skip = eye_t * (d_skip.astype(F32).reshape(n_groups, ch)[:, None, None, :, None]
                    * jnp.eye(ch, dtype=F32)[None, None, None])
    toep = (toep_f + toep_b + skip).transpose(0, 1, 3, 2, 4).reshape(n_groups, tc, tc)
    cf = jnp.einsum("gop,tgp->gpto", c_mat[0], pw[0, 1:t + 1]).reshape(n_groups, n_state, tc)
    cb = jnp.einsum("gop,tgp->gpto", c_mat[1], pw[1, t:0:-1]).reshape(n_groups, n_state, tc)
    zrow = jnp.zeros((n_groups, n_state, tc), F32)

    def pair_rows(m):
        even = jnp.concatenate([m, zrow], axis=1)
        odd = jnp.concatenate([zrow, m], axis=1)
        return jnp.where((jnp.arange(n_groups) % 2 == 0)[:, None, None], even, odd)

    w3 = jnp.concatenate([toep, pair_rows(jnp.real(cf)), pair_rows(-jnp.imag(cf)),
                          pair_rows(jnp.real(cb)), pair_rows(-jnp.imag(cb))], axis=1).astype(BF16)

    a_pow = pw[:, t]
    a_vec = jnp.stack([jnp.real(a_pow[0]), jnp.imag(a_pow[0]), jnp.real(a_pow[1]), jnp.imag(a_pow[1])],
                      axis=1).reshape(n_groups // 2, 2, 4, n_state)
    a_vec = a_vec.transpose(0, 2, 1, 3).reshape(n_groups // 2, 4, 2 * n_state)
    return w1, w3, a_vec


def _complex_step(a_re, a_im, h_re, h_im, s_re, s_im):
    return a_re * h_re - a_im * h_im + s_re, a_re * h_im + a_im * h_re + s_im


def _s5_scan(s_scr, h_scr, a, init, n_chunks, rows_per_chunk):
    pl_ = LANES
    if rows_per_chunk % SUBLANES == 0:
        rt = rows_per_chunk

        def body(c, carry):
            f_re, f_im, g_re, g_im = carry
            rf = pl.ds(pl.multiple_of(c * rt, SUBLANES), rt)
            rb = pl.ds(pl.multiple_of((n_chunks - 1 - c) * rt, SUBLANES), rt)
            h_scr[0, rf, :] = f_re
            h_scr[1, rf, :] = f_im
            h_scr[2, rb, :] = g_re
            h_scr[3, rb, :] = g_im
            f_re, f_im = _complex_step(a[0:1], a[1:2], f_re, f_im,
                                       s_scr[rf, 0:pl_], s_scr[rf, pl_:2 * pl_])
            g_re, g_im = _complex_step(a[2:3], a[3:4], g_re, g_im,
                                       s_scr[rb, 2 * pl_:3 * pl_], s_scr[rb, 3 * pl_:4 * pl_])
            return f_re, f_im, g_re, g_im

        return lax.fori_loop(0, n_chunks, body, tuple(init))

    assert rows_per_chunk * 2 == SUBLANES and n_chunks % 2 == 0
    half = rows_per_chunk
    n_tiles = n_chunks // 2
    low = lax.broadcasted_iota(I32, (SUBLANES, pl_), 0) < half

    def body(j, carry):
        f_re, f_im, g_re, g_im = carry
        rf = pl.ds(pl.multiple_of(j * SUBLANES, SUBLANES), SUBLANES)
        rb = pl.ds(pl.multiple_of((n_tiles - 1 - j) * SUBLANES, SUBLANES), SUBLANES)
        sf_re, sf_im = s_scr[rf, 0:pl_], s_scr[rf, pl_:2 * pl_]
        sb_re, sb_im = s_scr[rb, 2 * pl_:3 * pl_], s_scr[rb, 3 * pl_:4 * pl_]
        f1_re, f1_im = _complex_step(a[0:1], a[1:2], f_re, f_im, sf_re, sf_im)
        g1_re, g1_im = _complex_step(a[2:3], a[3:4], g_re, g_im, sb_re, sb_im)
        f1s_re, f1s_im = pltpu.roll(f1_re, half, 0), pltpu.roll(f1_im, half, 0)
        g1s_re, g1s_im = pltpu.roll(g1_re, half, 0), pltpu.roll(g1_im, half, 0)
        h_scr[0, rf, :] = jnp.where(low, f_re, f1s_re)
        h_scr[1, rf, :] = jnp.where(low, f_im, f1s_im)
        h_scr[2, rb, :] = jnp.where(low, g1s_re, g_re)
        h_scr[3, rb, :] = jnp.where(low, g1s_im, g_im)
        f2_re, f2_im = _complex_step(a[0:1], a[1:2], f1s_re, f1s_im, sf_re, sf_im)
        g2_re, g2_im = _complex_step(a[2:3], a[3:4], g1s_re, g1s_im, sb_re, sb_im)
        return (pltpu.roll(f2_re, half, 0), pltpu.roll(f2_im, half, 0),
                pltpu.roll(g2_re, half, 0), pltpu.roll(g2_im, half, 0))

    return lax.fori_loop(0, n_tiles, body, tuple(init))


def _s5_kernel(up_ref, us_ref, w1_ref, w3_ref, a_ref, h0_ref, gp_ref, gs_ref, fin_ref,
               sp_scr, ss_scr, hp_scr, hs_scr, *, geom):
    tc = up_ref.shape[-1]
    a = a_ref[0]
    w1 = [w1_ref[0, 0], w1_ref[0, 1]]
    w1_hi = [w.astype(BF16) for w in w1]

    sp_scr[...] = _dot3(up_ref[0], w1[0]) + _dot3(up_ref[1], w1[1])
    ss_scr[...] = _dot(us_ref[0].astype(BF16), w1_hi[0]) + _dot(us_ref[1].astype(BF16), w1_hi[1])

    zero = jnp.zeros((geom["bp"], LANES), F32)
    fin = _s5_scan(sp_scr, hp_scr, a, (zero, zero, zero, zero), geom["lp"] // S5_CHUNK, geom["bp"])
    for comp in range(4):
        fin_ref[0, comp] = fin[comp]
    _s5_scan(ss_scr, hs_scr, a, tuple(h0_ref[0, comp] for comp in range(4)),
             geom["ls"] // S5_CHUNK, geom["bs"])

    for u_ref, h_scr, o_ref in ((up_ref, hp_scr, gp_ref), (us_ref, hs_scr, gs_ref)):
        for gi in range(2):
            y = _dot(u_ref[gi].astype(BF16), w3_ref[gi, 0:tc, :])
            for comp in range(4):
                y += _dot(h_scr[comp].astype(BF16),
                          w3_ref[gi, tc + comp * LANES:tc + (comp + 1) * LANES, :])
            o_ref[gi] = _gelu_tanh(y).astype(o_ref.dtype)


def _s5_mixer(h, h0_re, h0_im, params, geom):
    m, d = h.shape
    t = S5_CHUNK
    ch = GROUP_CH
    n_groups = d // ch
    tc = t * ch
    bp, lp, bs, ls, mp = geom["bp"], geom["lp"], geom["bs"], geom["ls"], geom["mp"]
    rp, rs = mp // t, (m - mp) // t
    w1, w3, a_vec = _s5_operands(*params)
    n_state = a_vec.shape[-1] // 2

    def to_chunks(x, b, l):
        x = x.reshape(b, l // t, t, n_groups, ch).transpose(3, 1, 0, 2, 4)
        return x.reshape(n_groups, (l // t) * b, tc)

    def from_chunks(x, b, l):
        x = x.reshape(n_groups, l // t, b, t, ch).transpose(2, 1, 3, 0, 4)
        return x.reshape(b * l, d)

    up = to_chunks(h[:mp], bp, lp)
    us = to_chunks(h[mp:], bs, ls)
    h0 = jnp.stack([h0_re[:, 0], h0_im[:, 0], h0_re[:, 1], h0_im[:, 1]], axis=0).astype(F32)
    h0 = h0.reshape(4, bs, n_groups // 2, 2 * n_state).transpose(2, 0, 1, 3)
    reps = max(1, SUBLANES // bs)
    h0 = jnp.tile(h0, (1, 1, reps, 1))
    h0_rows = h0.shape[2]

    gp, gs, fin = pl.pallas_call(
        functools.partial(_s5_kernel, geom=geom),
        grid=(n_groups // 2,),
        in_specs=[
            pl.BlockSpec((2, rp, tc), lambda i: (i, 0, 0)),
            pl.BlockSpec((2, rs, tc), lambda i: (i, 0, 0)),
            pl.BlockSpec((1, 2, tc, 4 * LANES), lambda i: (i, 0, 0, 0)),
            pl.BlockSpec((2, tc + 4 * LANES, tc), lambda i: (i, 0, 0)),
            pl.BlockSpec((1, 4, LANES), lambda i: (i, 0, 0)),
            pl.BlockSpec((1, 4, h0_rows, LANES), lambda i: (i, 0, 0, 0)),
        ],
        out_specs=[
            pl.BlockSpec((2, rp, tc), lambda i: (i, 0, 0)),
            pl.BlockSpec((2, rs, tc), lambda i: (i, 0, 0)),
            pl.BlockSpec((1, 4, bp, LANES), lambda i: (i, 0, 0, 0)),
        ],
        out_shape=[
            jax.ShapeDtypeStruct((n_groups, rp, tc), BF16),
            jax.ShapeDtypeStruct((n_groups, rs, tc), BF16),
            jax.ShapeDtypeStruct((n_groups // 2, 4, bp, LANES), F32),
        ],
        scratch_shapes=[
            pltpu.VMEM((rp, 4 * LANES), F32),
            pltpu.VMEM((rs, 4 * LANES), F32),
            pltpu.VMEM((4, rp, LANES), F32),
            pltpu.VMEM((4, rs, LANES), F32),
        ],
        compiler_params=_cparams("arbitrary"),
        name="s5_mixer",
    )(up, us, w1, w3, a_vec, h0)

    g = jnp.concatenate([from_chunks(gp, bp, lp), from_chunks(gs, bs, ls)], axis=0)
    fin = fin.reshape(n_groups // 2, 2, 2, bp, 2, n_state).transpose(3, 1, 2, 0, 4, 5)
    fin = fin.reshape(bp, 2, 2, n_groups, n_state)
    return g, fin[:, :, 0], fin[:, :, 1]


def _softmax_with_sink(scores, sink):
    mx = sink
    for s in scores:
        mx = jnp.maximum(mx, jnp.max(s, axis=-1, keepdims=True))
    ps = [jnp.exp(s - mx) for s in scores]
    den = jnp.exp(sink - mx)
    for p in ps:
        den = den + jnp.sum(p, axis=-1, keepdims=True)
    return ps, 1.0 / den


def _attn_prompt_kernel(sink_ref, q_ref, kv_ref, o_ref, *, n_kv, q_per_kv):
    hd = HEAD_DIM
    scale = hd ** -0.5
    for kv in range(n_kv):
        k = kv_ref[:, kv * hd:(kv + 1) * hd].astype(BF16)
        v = kv_ref[:, (n_kv + kv) * hd:(n_kv + kv + 1) * hd].astype(BF16)
        for gq in range(q_per_kv):
            head = kv * q_per_kv + gq
            q = (q_ref[:, head * hd:(head + 1) * hd] * scale).astype(BF16)
            s = lax.dot_general(q, k, _NT, preferred_element_type=F32)
            (p,), inv = _softmax_with_sink([s], sink_ref[head])
            o_ref[:, head * hd:(head + 1) * hd] = (_dot(p.astype(BF16), v) * inv).astype(o_ref.dtype)


def _attn_prompt(qkv, sink, geom, n_kv):
    d = geom["d"]
    bp, lp = geom["bp"], geom["lp"]
    kvw = 2 * n_kv * HEAD_DIM
    n_heads = d // HEAD_DIM
    return pl.pallas_call(
        functools.partial(_attn_prompt_kernel, n_kv=n_kv, q_per_kv=n_heads // n_kv),
        grid=(bp,),
        in_specs=[
            pl.BlockSpec(memory_space=pltpu.SMEM),
            pl.BlockSpec((lp, d), lambda b: (b, 0)),
            pl.BlockSpec((lp, kvw), lambda b: (b, d // kvw)),
        ],
        out_specs=pl.BlockSpec((lp, d), lambda b: (b, 0)),
        out_shape=jax.ShapeDtypeStruct((bp * lp, d), BF16),
        compiler_params=_cparams("arbitrary"),
        name="attn_prompt",
    )(sink, qkv, qkv)


def _rope_tables(length):
    pairs = HEAD_DIM // 4
    pos = jnp.arange(length)
    row = (pos // GRID_W).astype(F32)
    col = (pos % GRID_W).astype(F32)
    inv_freq = ROPE_BASE ** (-jnp.arange(pairs, dtype=F32) / pairs)
    lane = jnp.arange(LANES) % HEAD_DIM
    axis_pos = jnp.where((lane // (2 * pairs))[None, :] == 0, row[:, None], col[:, None])
    ang = axis_pos * inv_freq[lane % pairs][None, :]
    sign = jnp.where((lane % (2 * pairs)) < pairs, -1.0, 1.0)[None, :]
    return jnp.cos(ang), jnp.sin(ang) * sign


def _rope(x, cos, sin):
    w = x.shape[1]
    pairs = HEAD_DIM // 4
    reps = w // LANES
    cos_w = jnp.concatenate([cos] * reps, axis=1) if reps > 1 else cos
    sin_w = jnp.concatenate([sin] * reps, axis=1) if reps > 1 else sin
    lane = lax.broadcasted_iota(I32, x.shape, 1)
    partner = jnp.where(lane % (2 * pairs) < pairs, pltpu.roll(x, w - pairs, 1), pltpu.roll(x, pairs, 1))
    return x * cos_w + partner * sin_w


def _attn_latent_kernel(sink_ref, q_ref, kvp_ref, kvo_ref, kvn_ref, ck_ref, cv_ref,
                        cos_p, sin_p, cos_o, sin_o, cos_n, sin_n, o_ref, *, n_kv, q_per_kv, n_blocks):
    hd = HEAD_DIM
    blk = ATTN_BLOCK
    scale = hd ** -0.5
    i = pl.program_id(1)
    kw = n_kv * hd
    k_win = jnp.concatenate([
        _rope(kvp_ref[:, 0:kw], cos_p[...], sin_p[...]),
        _rope(kvo_ref[:, 0:kw], cos_o[...], sin_o[...]),
        _rope(kvn_ref[:, 0:kw], cos_n[...], sin_n[...])], axis=0).astype(BF16)
    v_win = jnp.concatenate([kvp_ref[:, kw:2 * kw], kvo_ref[:, kw:2 * kw], kvn_ref[:, kw:2 * kw]],
                            axis=0).astype(BF16)
    rows = q_per_kv * blk
    r_in_blk = lax.broadcasted_iota(I32, (rows, 3 * blk), 0) % blk
    c_idx = lax.broadcasted_iota(I32, (rows, 3 * blk), 1)
    k_pos = (i - 1) * blk + c_idx
    mask = (jnp.abs(c_idx - blk - r_in_blk) <= WINDOW) & (k_pos >= 0) & (k_pos < n_blocks * blk)
    for kv in range(n_kv):
        qg = _rope(q_ref[:, kv * q_per_kv * hd:(kv + 1) * q_per_kv * hd], cos_o[...], sin_o[...]) * scale
        q = jnp.concatenate([qg[:, gq * hd:(gq + 1) * hd] for gq in range(q_per_kv)], axis=0).astype(BF16)
        sink = jnp.concatenate(
            [jnp.full((blk, 1), sink_ref[kv * q_per_kv + gq], F32) for gq in range(q_per_kv)], axis=0)
        s_ctx = lax.dot_general(q, ck_ref[0, :, kv * hd:(kv + 1) * hd].astype(BF16), _NT,
                                preferred_element_type=F32)
        s_win = lax.dot_general(q, k_win[:, kv * hd:(kv + 1) * hd], _NT, preferred_element_type=F32)
        s_win = jnp.where(mask, s_win, NEG_INF)
        (p_ctx, p_win), inv = _softmax_with_sink([s_ctx, s_win], sink)
        o = (_dot(p_ctx.astype(BF16), cv_ref[0, :, kv * hd:(kv + 1) * hd].astype(BF16))
             + _dot(p_win.astype(BF16), v_win[:, kv * hd:(kv + 1) * hd])) * inv
        for gq in range(q_per_kv):
            head = kv * q_per_kv + gq
            o_ref[:, head * hd:(head + 1) * hd] = o[gq * blk:(gq + 1) * blk].astype(o_ref.dtype)


def _attn_latent(qkv, ctx_k, ctx_v, sink, geom, n_kv):
    d = geom["d"]
    bs, ls, mp = geom["bs"], geom["ls"], geom["mp"]
    blk = ATTN_BLOCK
    nb = ls // blk
    kvw = 2 * n_kv * HEAD_DIM
    n_heads = d // HEAD_DIM
    past = ctx_k.shape[1]
    ck = ctx_k.reshape(bs, past, n_kv * HEAD_DIM).astype(F32)
    cv = ctx_v.reshape(bs, past, n_kv * HEAD_DIM).astype(F32)
    cos, sin = _rope_tables(ls)
    base = mp // blk

    def prev(i):
        return jnp.maximum(i - 1, 0)

    def nxt(i):
        return jnp.minimum(i + 1, nb - 1)

    def kv_spec(f):
        return pl.BlockSpec((blk, kvw), lambda b, i: (base + b * nb + f(i), d // kvw))

    def tab_spec(f):
        return pl.BlockSpec((blk, LANES), lambda b, i: (f(i), 0))

    same = lambda i: i
    return pl.pallas_call(
        functools.partial(_attn_latent_kernel, n_kv=n_kv, q_per_kv=n_heads // n_kv, n_blocks=nb),
        grid=(bs, nb),
        in_specs=[
            pl.BlockSpec(memory_space=pltpu.SMEM),
            pl.BlockSpec((blk, d), lambda b, i: (base + b * nb + i, 0)),
            kv_spec(prev), kv_spec(same), kv_spec(nxt),
            pl.BlockSpec((1, past, n_kv * HEAD_DIM), lambda b, i: (b, 0, 0)),
            pl.BlockSpec((1, past, n_kv * HEAD_DIM), lambda b, i: (b, 0, 0)),
            tab_spec(prev), tab_spec(prev), tab_spec(same), tab_spec(same), tab_spec(nxt), tab_spec(nxt),
        ],
        out_specs=pl.BlockSpec((blk, d), lambda b, i: (b * nb + i, 0)),
        out_shape=jax.ShapeDtypeStruct((bs * ls, d), BF16),
        compiler_params=_cparams("arbitrary", "arbitrary"),
        name="attn_latent",
    )(sink, qkv, qkv, qkv, qkv, ck, cv, cos, sin, cos, sin, cos, sin)


def _pack_bf16_pairs(x):
    n = x.shape[1] // 2
    xb = x.astype(BF16).astype(F32)
    hi = lax.bitcast_convert_type(xb[:, :n], U32)
    lo = lax.bitcast_convert_type(xb[:, n:], U32)
    return hi | (lo >> 16)


def _store_row_tiles(ref, words):
    for s in range(ref.shape[1]):
        ref[:, s, :] = words[:, s * LANES:(s + 1) * LANES]


def _unpack_hi(words):
    return lax.bitcast_convert_type(words & jnp.uint32(0xFFFF0000), F32)


def _unpack_lo(words):
    return lax.bitcast_convert_type(words << 16, F32)


def _load_row_tiles_bf16(ref):
    tiles = [ref[:, s, :] for s in range(ref.shape[1])]
    return jnp.concatenate([_unpack_hi(t).astype(BF16) for t in tiles]
                           + [_unpack_lo(t).astype(BF16) for t in tiles], axis=1)


def _router_kernel(x_ref, g_ref, sc_ref, sh_ref, wt_hi_ref, wt_lo_ref, bias_ref,
                   h3_ref, lposd_ref, lposp_ref, w_ref, cnt_ref, base_ref, base_scr, *, n_experts):
    step = pl.program_id(0)

    @pl.when(step == 0)
    def _():
        base_scr[...] = jnp.zeros_like(base_scr)

    h = _norm_mod(x_ref[...], g_ref[...], sc_ref[0], sh_ref[0])
    _store_row_tiles(h3_ref, _pack_bf16_pairs(h))
    tm = h.shape[0]
    h_hi, h_lo = _split(h)
    wt_hi = wt_hi_ref[...]
    logits = (lax.dot_general(wt_hi, h_hi, _NT, preferred_element_type=F32)
              + lax.dot_general(wt_lo_ref[...], h_hi, _NT, preferred_element_type=F32)
              + lax.dot_general(wt_hi, h_lo, _NT, preferred_element_type=F32))
    scores = _sigmoid(logits)
    sel = scores + bias_ref[:, 0:1]
    per_group = n_experts // N_EXPERT_GROUPS
    e_iota = lax.broadcasted_iota(I32, (n_experts, tm), 0)
    big = jnp.int32(1 << 30)

    grp = []
    for gidx in range(N_EXPERT_GROUPS):
        v = sel[gidx * per_group:(gidx + 1) * per_group]
        r = lax.broadcasted_iota(I32, v.shape, 0) + gidx * per_group
        m1 = jnp.max(v, axis=0, keepdims=True)
        i1 = jnp.min(jnp.where(v == m1, r, big), axis=0, keepdims=True)
        m2 = jnp.max(jnp.where(r == i1, -jnp.inf, v), axis=0, keepdims=True)
        grp.append(jnp.broadcast_to(m1 + m2, v.shape))
    cur = jnp.concatenate(grp, axis=0)
    g_iota = e_iota // per_group
    cand = jnp.full(sel.shape, -jnp.inf, F32)
    for _ in range(TOPK_GROUPS):
        mx = jnp.max(cur, axis=0, keepdims=True)
        gi = jnp.min(jnp.where(cur == mx, g_iota, big), axis=0, keepdims=True)
        hit = g_iota == gi
        cand = jnp.where(hit, sel, cand)
        cur = jnp.where(hit, -jnp.inf, cur)

    idxs, wts = [], []
    onehot = jnp.zeros((n_experts, tm), F32)
    for _ in range(TOP_K):
        mx = jnp.max(cand, axis=0, keepdims=True)
        ei = jnp.min(jnp.where(cand == mx, e_iota, big), axis=0, keepdims=True)
        hit = e_iota == ei
        idxs.append(ei)
        wts.append(jnp.sum(jnp.where(hit, scores, 0.0), axis=0, keepdims=True))
        onehot = jnp.where(hit, 1.0, onehot)
        cand = jnp.where(hit, -jnp.inf, cand)
    wsum = wts[0]
    for wk in wts[1:]:
        wsum = wsum + wk
    w_ref[0] = jnp.concatenate(wts, axis=0) / wsum * ROUTED_SCALE

    onehot_b = onehot.astype(BF16)
    t_row = lax.broadcasted_iota(I32, (tm, tm), 0)
    t_col = lax.broadcasted_iota(I32, (tm, tm), 1)
    earlier_tok = _dot(onehot_b, jnp.where(t_row < t_col, 1.0, 0.0).astype(BF16))
    cnt = jnp.sum(onehot, axis=1, keepdims=True)
    cnt_pad = jnp.floor((cnt + (DMA_WINDOW - 1)) * (1.0 / DMA_WINDOW)) * DMA_WINDOW
    e_row = lax.broadcasted_iota(I32, (n_experts, n_experts), 0)
    e_col = lax.broadcasted_iota(I32, (n_experts, n_experts), 1)
    lower = jnp.where(e_col < e_row, 1.0, 0.0).astype(BF16)
    start_d = _dot(lower, jnp.broadcast_to(cnt, (n_experts, LANES)).astype(BF16))[:, 0:1]
    start_p = _dot(lower, jnp.broadcast_to(cnt_pad, (n_experts, LANES)).astype(BF16))[:, 0:1]
    pos_d = start_d + earlier_tok
    pos_p = start_p + earlier_tok
    lposd_ref[0] = jnp.concatenate(
        [jnp.sum(jnp.where(e_iota == ei, pos_d, 0.0), axis=0, keepdims=True) for ei in idxs], axis=0).astype(I32)
    lposp_ref[0] = jnp.concatenate(
        [jnp.sum(jnp.where(e_iota == ei, pos_p, 0.0), axis=0, keepdims=True) for ei in idxs], axis=0).astype(I32)

    cnt_row = lax.dot_general(jnp.ones((SUBLANES, tm), BF16), onehot_b, _NT, preferred_element_type=F32)
    cnt_ref[0] = cnt_row
    base_ref[0] = base_scr[...]
    base_scr[...] = base_scr[...] + cnt_row


def _route(x, g, sc, sh, w_router, bias, geom):
    m, d = x.shape
    n_experts = w_router.shape[1]
    tm = MOE_TILE
    n_tiles = m // tm
    ns = d // (2 * LANES)
    wt = w_router.astype(F32).T
    wt_hi = wt.astype(BF16)
    wt_lo = (wt - wt_hi.astype(F32)).astype(BF16)
    bias_b = jnp.broadcast_to(bias.astype(F32)[:, None], (n_experts, LANES))
    midx = functools.partial(_mod_index, n_prompt_tiles=geom["mp"] // tm,
                             tiles_per_latent_seq=geom["ls"] // tm)
    tok_spec = pl.BlockSpec((1, TOP_K, tm), lambda i: (i, 0, 0))
    exp_spec = pl.BlockSpec((1, SUBLANES, n_experts), lambda i: (i, 0, 0))
    return pl.pallas_call(
        functools.partial(_router_kernel, n_experts=n_experts),
        grid=(n_tiles,),
        in_specs=[
            pl.BlockSpec((tm, d), lambda i: (i, 0)),
            pl.BlockSpec((1, d), lambda i: (0, 0)),
            pl.BlockSpec((1, 1, d), lambda i: (midx(i), 0, 0)),
            pl.BlockSpec((1, 1, d), lambda i: (midx(i), 0, 0)),
            pl.BlockSpec((n_experts, d), lambda i: (0, 0)),
            pl.BlockSpec((n_experts, d), lambda i: (0, 0)),
            pl.BlockSpec((n_experts, LANES), lambda i: (0, 0)),
        ],
        out_specs=[
            pl.BlockSpec((tm, ns, LANES), lambda i: (i, 0, 0)),
            tok_spec, tok_spec, tok_spec, exp_spec, exp_spec,
        ],
        out_shape=[
            jax.ShapeDtypeStruct((m, ns, LANES), U32),
            jax.ShapeDtypeStruct((n_tiles, TOP_K, tm), I32),
            jax.ShapeDtypeStruct((n_tiles, TOP_K, tm), I32),
            jax.ShapeDtypeStruct((n_tiles, TOP_K, tm), F32),
            jax.ShapeDtypeStruct((n_tiles, SUBLANES, n_experts), F32),
            jax.ShapeDtypeStruct((n_tiles, SUBLANES, n_experts), F32),
        ],
        scratch_shapes=[pltpu.VMEM((SUBLANES, n_experts), F32)],
        compiler_params=_cparams("arbitrary"),
        name="moe_router",
    )(x, g.reshape(1, d), sc, sh, wt_hi, wt_lo, bias_b)


_FILL_SIZES = tuple(1 << b for b in range(EXPERT_TILE.bit_length() - 1, -1, -1))


def _dispatch_kernel(fill_start_ref, fill_n_ref, lpos_hbm, scal_hbm, h3_ref, xs_hbm,
                     lpos_smem, scal_smem, buf, zero_scr, idx_sem, row_sem, *, n_experts):
    step = pl.program_id(0)
    tm = h3_ref.shape[0]
    n_assign = TOP_K * tm
    win = DMA_WINDOW

    @pl.when(step == 0)
    def _():
        zero_scr[...] = jnp.zeros_like(zero_scr)
        buf[...] = jnp.zeros_like(buf)

        def fill_copies(e):
            start = fill_start_ref[e]
            n = fill_n_ref[e]
            off = jnp.int32(0)
            out = []
            for size in _FILL_SIZES:
                take = (n & size) != 0
                out.append((take, pltpu.make_async_copy(zero_scr.at[pl.ds(0, size)],
                                                        xs_hbm.at[pl.ds(start + off, size)], row_sem)))
                off = off + jnp.where(take, size, 0)
            return out

        def issue(e, carry):
            for take, cp in fill_copies(e):
                @pl.when(take)
                def _():
                    cp.start()
            return carry

        def drain(e, carry):
            for take, cp in fill_copies(e):
                @pl.when(take)
                def _():
                    cp.wait()
            return carry

        lax.fori_loop(0, n_experts, issue, 0)
        lax.fori_loop(0, n_experts, drain, 0)

    @pl.when(step > 0)
    def _():
        tile = step - 1
        idx_copies = [pltpu.make_async_copy(lpos_hbm.at[tile], lpos_smem, idx_sem.at[0]),
                      pltpu.make_async_copy(scal_hbm.at[tile], scal_smem, idx_sem.at[1])]
        for cp in idx_copies:
            cp.start()
        for cp in idx_copies:
            cp.wait()

        def per_token(t, carry):
            row = h3_ref[t]
            for k in range(TOP_K):
                buf[lpos_smem[k * tm + t]] = row
            return carry

        lax.fori_loop(0, tm, per_token, 0)

        def window_copy(src, dst):
            return pltpu.make_async_copy(buf.at[pl.ds(src, win)], xs_hbm.at[pl.ds(dst, win)], row_sem)

        def per_expert(e, carry):
            loc, n_win = carry
            n = scal_smem[e]
            dst = scal_smem[n_experts + e]
            nw = (n + (win - 1)) // win

            def one(j, c):
                window_copy(loc + j * win, dst + j * win).start()
                return c

            lax.fori_loop(0, nw, one, 0)
            return loc + n, n_win + nw

        _, n_win = lax.fori_loop(0, n_experts, per_expert, (jnp.int32(0), jnp.int32(0)))

        def drain(j, carry):
            window_copy(0, 0).wait()
            return carry

        lax.fori_loop(0, n_win, drain, 0)


def _dispatch(h3, lposd, scal, fill_start, fill_n, n_slots):
    m, ns, _ = h3.shape
    tm = MOE_TILE
    n_tiles = m // tm
    n_experts = fill_start.shape[0]

    def tile_of(i, fs, fn):
        return jnp.maximum(i - 1, 0)

    return pl.pallas_call(
        functools.partial(_dispatch_kernel, n_experts=n_experts),
        grid_spec=pltpu.PrefetchScalarGridSpec(
            num_scalar_prefetch=2,
            grid=(n_tiles + 1,),
            in_specs=[
                pl.BlockSpec(memory_space=pl.ANY),
                pl.BlockSpec(memory_space=pl.ANY),
                pl.BlockSpec((tm, ns, LANES), lambda i, fs, fn: (tile_of(i, fs, fn), 0, 0)),
            ],
            out_specs=pl.BlockSpec(memory_space=pl.ANY),
            scratch_shapes=[
                pltpu.SMEM((TOP_K * tm,), I32),
                pltpu.SMEM((2 * n_experts,), I32),
                pltpu.VMEM((TOP_K * tm + DMA_WINDOW, ns, LANES), U32),
                pltpu.VMEM((EXPERT_TILE, ns, LANES), U32),
                pltpu.SemaphoreType.DMA((2,)),
                pltpu.SemaphoreType.DMA,
            ],
        ),
        out_shape=jax.ShapeDtypeStruct((n_slots, ns, LANES), U32),
        compiler_params=_cparams("arbitrary"),
        name="moe_dispatch",
    )(fill_start, fill_n, lposd.reshape(n_tiles, TOP_K * tm), scal, h3)


def _expert_kernel(be_ref, first_ref, nxt_ref, par_ref, na_ref, xs_ref, wg_hbm, wu_hbm, wd_hbm, y_ref,
                   wg_buf, wu_buf, wd_buf, wg_scr, wu_scr, wd_scr, sem, *, layer):
    i = pl.program_id(0)

    def fetch(expert, slot):
        return [pltpu.make_async_copy(src.at[layer, expert], dst.at[slot], sem.at[slot, j])
                for j, (src, dst) in enumerate(((wg_hbm, wg_buf), (wu_hbm, wu_buf), (wd_hbm, wd_buf)))]

    @pl.when(i < na_ref[0])
    def _():
        slot = par_ref[i]

        @pl.when(i == 0)
        def _():
            for cp in fetch(be_ref[0], slot):
                cp.start()

        @pl.when(first_ref[i] == 1)
        def _():
            @pl.when(nxt_ref[i] >= 0)
            def _():
                for cp in fetch(nxt_ref[i], 1 - slot):
                    cp.start()

            for cp in fetch(be_ref[i], slot):
                cp.wait()
            wg_scr[...] = wg_buf[slot].astype(BF16)
            wu_scr[...] = wu_buf[slot].astype(BF16)
            wd_scr[...] = wd_buf[slot].astype(BF16)

        x = _load_row_tiles_bf16(xs_ref)
        gate = _dot(x, wg_scr[...])
        up = _dot(x, wu_scr[...])
        act = (gate * _sigmoid(gate) * up).astype(BF16)
        _store_row_tiles(y_ref, _pack_bf16_pairs(_dot(act, wd_scr[...])))


def _experts(xs3, sched, w_gate, w_up, w_down, layer):
    n_slots, ns, _ = xs3.shape
    tm = EXPERT_TILE
    d, de = w_gate.shape[-2:]
    n_blocks = n_slots // tm

    def blk(i, be, first, nxt, par, na):
        return (jnp.minimum(i, na[0] - 1), 0, 0)

    return pl.pallas_call(
        functools.partial(_expert_kernel, layer=layer),
        grid_spec=pltpu.PrefetchScalarGridSpec(
            num_scalar_prefetch=5,
            grid=(n_blocks,),
            in_specs=[
                pl.BlockSpec((tm, ns, LANES), blk),
                pl.BlockSpec(memory_space=pl.ANY),
                pl.BlockSpec(memory_space=pl.ANY),
                pl.BlockSpec(memory_space=pl.ANY),
            ],
            out_specs=pl.BlockSpec((tm, ns, LANES), blk),
            scratch_shapes=[
                pltpu.VMEM((2, d, de), F32),
                pltpu.VMEM((2, d, de), F32),
                pltpu.VMEM((2, de, d), F32),
                pltpu.VMEM((d, de), BF16),
                pltpu.VMEM((d, de), BF16),
                pltpu.VMEM((de, d), BF16),
                pltpu.SemaphoreType.DMA((2, 3)),
            ],
        ),
        out_shape=jax.ShapeDtypeStruct((n_slots, ns, LANES), U32),
        compiler_params=_cparams("arbitrary"),
        name="moe_experts",
    )(*sched, xs3, w_gate, w_up, w_down)


def _combine_kernel(lpos_hbm, w_hbm, scal_hbm, y_hbm, x_ref, h3_ref, g_ref, sg_ref, su_ref, sd_ref, o_ref,
                    lpos_smem, w_smem, scal_smem, ybuf, rout, idx_sem, row_sem, *, n_experts):
    step = pl.program_id(0)
    tm = x_ref.shape[0]
    ns = ybuf.shape[1]
    win = DMA_WINDOW
    idx_copies = [pltpu.make_async_copy(lpos_hbm.at[step], lpos_smem, idx_sem.at[0]),
                  pltpu.make_async_copy(w_hbm.at[step], w_smem, idx_sem.at[1]),
                  pltpu.make_async_copy(scal_hbm.at[step], scal_smem, idx_sem.at[2])]
    for cp in idx_copies:
        cp.start()

    @pl.when(step == 0)
    def _():
        ybuf[...] = jnp.zeros_like(ybuf)

    for cp in idx_copies:
        cp.wait()

    def window_copy(src, dst):
        return pltpu.make_async_copy(y_hbm.at[pl.ds(src, win)], ybuf.at[pl.ds(dst, win)], row_sem)

    def per_expert(e, carry):
        loc, n_win = carry
        n = scal_smem[e]
        src = scal_smem[n_experts + e]
        nw = (n + (win - 1)) // win

        def one(j, c):
            window_copy(src + j * win, loc + j * win).start()
            return c

        lax.fori_loop(0, nw, one, 0)
        return loc + nw * win, n_win + nw

    _, n_win = lax.fori_loop(0, n_experts, per_expert, (jnp.int32(0), jnp.int32(0)))

    hb = _load_row_tiles_bf16(h3_ref)
    gate = _dot(hb, sg_ref[...])
    up = _dot(hb, su_ref[...])
    shared = _dot((gate * _sigmoid(gate) * up).astype(BF16), sd_ref[...])

    def drain(j, carry):
        window_copy(0, 0).wait()
        return carry

    lax.fori_loop(0, n_win, drain, 0)

    def per_token(t, carry):
        acc_hi = jnp.zeros((ns, LANES), F32)
        acc_lo = jnp.zeros((ns, LANES), F32)
        for k in range(TOP_K):
            words = ybuf[lpos_smem[k * tm + t]]
            wk = w_smem[k * tm + t]
            acc_hi = acc_hi + wk * _unpack_hi(words)
            acc_lo = acc_lo + wk * _unpack_lo(words)
        rout[t, 0:ns, :] = acc_hi
        rout[t, ns:2 * ns, :] = acc_lo
        return carry

    lax.fori_loop(0, tm, per_token, 0)
    routed = jnp.concatenate([rout[:, s, :] for s in range(2 * ns)], axis=1)
    o_ref[...] = x_ref[...] + g_ref[0] * (routed + shared)


def _combine(lposp, w_sel, scal, y3, x, h3, gate, s_gate, s_up, s_down, geom, layer):
    m, d = x.shape
    tm = MOE_TILE
    n_tiles = m // tm
    de = s_gate.shape[-1]
    ns = y3.shape[1]
    n_experts = scal.shape[1] // 2
    n_assign = TOP_K * tm
    midx = functools.partial(_mod_index, n_prompt_tiles=geom["mp"] // tm,
                             tiles_per_latent_seq=geom["ls"] // tm)
    return pl.pallas_call(
        functools.partial(_combine_kernel, n_experts=n_experts),
        grid=(n_tiles,),
        in_specs=[
            pl.BlockSpec(memory_space=pl.ANY),
            pl.BlockSpec(memory_space=pl.ANY),
            pl.BlockSpec(memory_space=pl.ANY),
            pl.BlockSpec(memory_space=pl.ANY),
            pl.BlockSpec((tm, d), lambda i: (i, 0)),
            pl.BlockSpec((tm, ns, LANES), lambda i: (i, 0, 0)),
            pl.BlockSpec((1, 1, d), lambda i: (midx(i), 0, 0)),
            pl.BlockSpec((None, d, de), lambda i: (layer, 0, 0)),
            pl.BlockSpec((None, d, de), lambda i: (layer, 0, 0)),
            pl.BlockSpec((None, de, d), lambda i: (layer, 0, 0)),
        ],
        out_specs=pl.BlockSpec((tm, d), lambda i: (i, 0)),
        out_shape=jax.ShapeDtypeStruct((m, d), F32),
        scratch_shapes=[
            pltpu.SMEM((n_assign,), I32),
            pltpu.SMEM((n_assign,), F32),
            pltpu.SMEM((2 * n_experts,), I32),
            pltpu.VMEM((n_assign + n_experts * (DMA_WINDOW - 1), ns, LANES), U32),
            pltpu.VMEM((tm, 2 * ns, LANES), F32),
            pltpu.SemaphoreType.DMA((3,)),
            pltpu.SemaphoreType.DMA,
        ],
        compiler_params=_cparams("arbitrary"),
        name="moe_combine",
    )(lposp.reshape(n_tiles, n_assign), w_sel.reshape(n_tiles, n_assign), scal, y3, x, h3, gate,
      s_gate, s_up, s_down)


def _moe_layer(x, norm_g, sc, sh, gate, w_router, bias, w_gate, w_up, w_down, s_gate, s_up, s_down,
               geom, layer):
    m, d = x.shape
    n_experts = w_router.shape[1]
    tm = EXPERT_TILE
    spare = DMA_WINDOW - 1
    h3, lposd, lposp, w_sel, tile_cnt, tile_base = _route(x, norm_g, sc, sh, w_router, bias, geom)

    tile_cnt = tile_cnt[:, 0, :].astype(I32)
    tile_base = tile_base[:, 0, :].astype(I32)
    counts = tile_base[-1] + tile_cnt[-1]
    padded = jnp.where(counts > 0, (counts + spare + tm - 1) // tm * tm, 0)
    pad_end = jnp.cumsum(padded)
    pad_start = pad_end - padded
    n_blocks = -(-(m * TOP_K + n_experts * (spare + tm - 1)) // tm)
    n_active = pad_end[-1] // tm
    blocks = jnp.arange(n_blocks, dtype=I32)
    block_expert = jnp.minimum(jnp.searchsorted(pad_end, blocks * tm, side="right"), n_experts - 1).astype(I32)
    prev_expert = jnp.concatenate([jnp.full((1,), -1, I32), block_expert[:-1]])
    first = (block_expert != prev_expert).astype(I32)
    parity = (jnp.cumsum(first) - 1) % 2
    next_block = pad_end[block_expert] // tm
    next_expert = jnp.where(next_block < n_active, block_expert[jnp.minimum(next_block, n_blocks - 1)], -1)
    sched = (block_expert, first, next_expert.astype(I32), parity.astype(I32), n_active.astype(I32).reshape(1))
    scal = jnp.concatenate([tile_cnt, pad_start[None, :] + tile_base], axis=1)

    xs3 = _dispatch(h3, lposd, scal, pad_start + counts, padded - counts, n_blocks * tm)
    y3 = _experts(xs3, sched, w_gate, w_up, w_down, layer)
    return _combine(lposp, w_sel, scal, y3, x, h3, gate, s_gate.astype(BF16), s_up.astype(BF16),
                    s_down.astype(BF16), geom, layer)


def kernel(x_prompt, x_sample, c, state_ssm_re, state_ssm_im, cache_k, cache_v, c_ctx, ada_w, ada_b, norm1_g, norm2_g, final_norm_g, s5_lam_re, s5_lam_im, s5_log_step, s5_b_re, s5_b_im, s5_c_re, s5_c_im, s5_d, s5_w_glu, attn_w_qkv, attn_b_qkv, attn_w_o, attn_sink, moe_w_router, moe_router_bias, moe_w_gate, moe_w_up, moe_w_down, moe_shared_gate, moe_shared_up, moe_shared_down):
    bp, lp, d = x_prompt.shape
    bs, ls, _ = x_sample.shape
    depth = ada_w.shape[0]
    n_kv = cache_k.shape[3]
    mp, ms = bp * lp, bs * ls
    geom = dict(bp=bp, lp=lp, bs=bs, ls=ls, mp=mp, d=d)
    assert lp % ROW_TILE == 0 and ls % ROW_TILE == 0 and bs in (4,) and bp % SUBLANES == 0
    assert (2 * n_kv * HEAD_DIM) <= d and d % (2 * n_kv * HEAD_DIM) == 0

    x = jnp.concatenate([x_prompt.reshape(mp, d), x_sample.reshape(ms, d)], axis=0).astype(F32)

    n_mod = -(-(bs + 1) // SUBLANES) * SUBLANES
    cvec = jnp.zeros((n_mod, d), F32).at[0].set(c_ctx.astype(F32)).at[1:bs + 1].set(c.astype(F32))
    mods = _ada_params(cvec, ada_w, ada_b).reshape(depth, n_mod, 6, 1, d)

    new_re, new_im, new_k, new_v = [], [], [], []
    for i in range(depth):
        j = i // N_MIXERS
        sh1, sc1, g1, sh2, sc2, g2 = (mods[i, :, part] for part in range(6))
        if i % N_MIXERS == 0:
            h = _norm_mod_call(x, norm1_g[i], sc1, sh1, geom, F32)
            params = (s5_lam_re[j], s5_lam_im[j], s5_log_step[j], s5_b_re[j], s5_b_im[j],
                      s5_c_re[j], s5_c_im[j], s5_d[j])
            g, st_re, st_im = _s5_mixer(h, state_ssm_re[:, j], state_ssm_im[:, j], params, geom)
            new_re.append(st_re)
            new_im.append(st_im)
            x = _mm_resid(g, s5_w_glu[j].astype(BF16), x, g1, geom, glu=True)
        else:
            h = _norm_mod_call(x, norm1_g[i], sc1, sh1, geom, BF16)
            qkv = _mm_bias(h, attn_w_qkv[j].astype(BF16), attn_b_qkv[j].astype(F32))
            sink = attn_sink[j].astype(F32)
            o = jnp.concatenate([
                _attn_prompt(qkv, sink, geom, n_kv),
                _attn_latent(qkv, cache_k[:, j], cache_v[:, j], sink, geom, n_kv)], axis=0)
            kw = n_kv * HEAD_DIM
            new_k.append(qkv[:mp, d:d + kw].reshape(bp, lp, n_kv, HEAD_DIM))
            new_v.append(qkv[:mp, d + kw:d + 2 * kw].reshape(bp, lp, n_kv, HEAD_DIM))
            x = _mm_resid(o, attn_w_o[j].astype(BF16), x, g1, geom, glu=False)
        x = _moe_layer(x, norm2_g[i], sc2, sh2, g2, moe_w_router[i], moe_router_bias[i],
                       moe_w_gate, moe_w_up, moe_w_down, moe_shared_gate, moe_shared_up, moe_shared_down,
                       geom, i)

    zeros = jnp.zeros((n_mod, 1, d), F32)
    y = _norm_mod_call(x, final_norm_g, zeros, zeros, geom, F32)
    return (y[:mp].reshape(bp, lp, d), y[mp:].reshape(bs, ls, d),
            jnp.stack(new_re, axis=1), jnp.stack(new_im, axis=1),
            jnp.stack(new_k, axis=1), jnp.stack(new_v, axis=1))
```

```python
import functools
import math

import jax
import jax.numpy as jnp
from jax import lax
from jax.experimental import pallas as pl
from jax.experimental.pallas import tpu as pltpu

F32 = jnp.float32
BF16 = jnp.bfloat16
I32 = jnp.int32

GROUP_CH = 16
HEAD_DIM = 64
WINDOW = 128
ATTN_BLOCK = 128
GRID_W = 64
ROPE_BASE = 10000.0
TOP_K = 8
N_EXPERT_GROUPS = 8
TOPK_GROUPS = 4
ROUTED_SCALE = 2.5
RMS_EPS = 1e-6
NEG_INF = -1e30
N_MIXERS = 2

LANES = 128
SUBLANES = 8
VMEM_LIMIT_BYTES = 56 * 1024 * 1024

S5_CHUNK = 16
ROW_TILE = 256
EXPERT_TILE = 256
MOE_TILE = 256
DMA_WINDOW = 8
U32 = jnp.uint32

_NT = (((1,), (1,)), ((), ()))


def _cparams(*sem):
    return pltpu.CompilerParams(dimension_semantics=sem, vmem_limit_bytes=VMEM_LIMIT_BYTES)


def _dot(a, b):
    return jnp.dot(a, b, preferred_element_type=F32)


def _split(x):
    hi = x.astype(BF16)
    lo = (x - hi.astype(F32)).astype(BF16)
    return hi, lo


def _dot3(a, b):
    a_hi, a_lo = _split(a)
    b_hi, b_lo = _split(b)
    return _dot(a_hi, b_hi) + _dot(a_lo, b_hi) + _dot(a_hi, b_lo)


def _sigmoid(x):
    return 1.0 / (1.0 + jnp.exp(-x))


def _gelu_tanh(x):
    c = math.sqrt(2.0 / math.pi)
    return 0.5 * x * (1.0 + jnp.tanh(c * (x + 0.044715 * (x * x * x))))


def _mod_index(tile, n_prompt_tiles, tiles_per_latent_seq):
    return jnp.where(tile < n_prompt_tiles, 0, 1 + (tile - n_prompt_tiles) // tiles_per_latent_seq)


def _ada_kernel(c_ref, w_ref, b_ref, o_ref):
    c = c_ref[...]
    o_ref[0] = _dot3(c * _sigmoid(c), w_ref[0]) + b_ref[0]


def _ada_params(cvec, ada_w, ada_b):
    depth, d, n = ada_w.shape
    rows = cvec.shape[0]
    tn = min(512, n)
    return pl.pallas_call(
        _ada_kernel,
        grid=(depth, n // tn),
        in_specs=[
            pl.BlockSpec((rows, d), lambda l, j: (0, 0)),
            pl.BlockSpec((1, d, tn), lambda l, j: (l, 0, j)),
            pl.BlockSpec((1, 1, tn), lambda l, j: (l, 0, j)),
        ],
        out_specs=pl.BlockSpec((1, rows, tn), lambda l, j: (l, 0, j)),
        out_shape=jax.ShapeDtypeStruct((depth, rows, n), F32),
        compiler_params=_cparams("arbitrary", "arbitrary"),
        name="ada_params",
    )(cvec, ada_w, ada_b.reshape(depth, 1, n))


def _norm_mod(x, g, sc, sh):
    ms = jnp.mean(x * x, axis=-1, keepdims=True)
    return (x * lax.rsqrt(ms + RMS_EPS) * g) * (1.0 + sc) + sh


def _norm_mod_kernel(x_ref, g_ref, sc_ref, sh_ref, o_ref):
    o_ref[...] = _norm_mod(x_ref[...], g_ref[...], sc_ref[0], sh_ref[0]).astype(o_ref.dtype)


def _norm_mod_call(x, g, sc, sh, geom, out_dtype):
    m, d = x.shape
    tm = ROW_TILE
    midx = functools.partial(_mod_index, n_prompt_tiles=geom["mp"] // tm,
                             tiles_per_latent_seq=geom["ls"] // tm)
    return pl.pallas_call(
        _norm_mod_kernel,
        grid=(m // tm,),
        in_specs=[
            pl.BlockSpec((tm, d), lambda i: (i, 0)),
            pl.BlockSpec((1, d), lambda i: (0, 0)),
            pl.BlockSpec((1, 1, d), lambda i: (midx(i), 0, 0)),
            pl.BlockSpec((1, 1, d), lambda i: (midx(i), 0, 0)),
        ],
        out_specs=pl.BlockSpec((tm, d), lambda i: (i, 0)),
        out_shape=jax.ShapeDtypeStruct((m, d), out_dtype),
        compiler_params=_cparams("arbitrary"),
        name="norm_mod",
    )(x, g.reshape(1, d), sc, sh)


def _mm_bias_kernel(x_ref, w_ref, b_ref, o_ref):
    o_ref[...] = _dot(x_ref[...], w_ref[...]) + b_ref[...]


def _mm_bias(x, w, b):
    m, k = x.shape
    n = w.shape[1]
    tm = min(512, m)
    tn = min(512, n)
    return pl.pallas_call(
        _mm_bias_kernel,
        grid=(n // tn, m // tm),
        in_specs=[
            pl.BlockSpec((tm, k), lambda j, i: (i, 0)),
            pl.BlockSpec((k, tn), lambda j, i: (0, j)),
            pl.BlockSpec((1, tn), lambda j, i: (0, j)),
        ],
        out_specs=pl.BlockSpec((tm, tn), lambda j, i: (i, j)),
        out_shape=jax.ShapeDtypeStruct((m, n), F32),
        compiler_params=_cparams("arbitrary", "arbitrary"),
        name="mm_bias",
    )(x, w, b.reshape(1, n))


def _mm_resid_kernel(a_ref, w_ref, x_ref, g_ref, o_ref):
    o_ref[...] = x_ref[...] + g_ref[0] * _dot(a_ref[...], w_ref[...])


def _mm_glu_resid_kernel(a_ref, wv_ref, wg_ref, x_ref, g_ref, o_ref):
    a = a_ref[...]
    val = _dot(a, wv_ref[...])
    gate = _dot(a, wg_ref[...])
    o_ref[...] = x_ref[...] + g_ref[0] * (val * _sigmoid(gate))


def _mm_resid(a, w, x, gate, geom, glu):
    m, k = a.shape
    n = x.shape[1]
    tm = ROW_TILE
    tn = min(1024, n)
    nj = n // tn
    midx = functools.partial(_mod_index, n_prompt_tiles=geom["mp"] // tm,
                             tiles_per_latent_seq=geom["ls"] // tm)
    a_spec = pl.BlockSpec((tm, k), lambda j, i: (i, 0))
    w_spec = pl.BlockSpec((k, tn), lambda j, i: (0, j))
    x_spec = pl.BlockSpec((tm, tn), lambda j, i: (i, j))
    g_spec = pl.BlockSpec((1, 1, tn), lambda j, i: (midx(i), 0, j))
    if glu:
        body = _mm_glu_resid_kernel
        in_specs = [a_spec, w_spec, pl.BlockSpec((k, tn), lambda j, i: (0, j + nj)), x_spec, g_spec]
        args = (a, w, w, x, gate)
    else:
        body = _mm_resid_kernel
        in_specs = [a_spec, w_spec, x_spec, g_spec]
        args = (a, w, x, gate)
    return pl.pallas_call(
        body,
        grid=(nj, m // tm),
        in_specs=in_specs,
        out_specs=pl.BlockSpec((tm, tn), lambda j, i: (i, j)),
        out_shape=jax.ShapeDtypeStruct((m, n), F32),
        compiler_params=_cparams("arbitrary", "arbitrary"),
        name="mm_glu_resid" if glu else "mm_resid",
    )(*args)


def _s5_operands(lam_re, lam_im, log_step, b_re, b_im, c_re, c_im, d_skip):
    t = S5_CHUNK
    n_groups, n_state = lam_re.shape[1:]
    ch = GROUP_CH
    tc = t * ch
    lam = lax.complex(jnp.minimum(lam_re.astype(F32), -1e-4), lam_im.astype(F32))
    step = jnp.exp(log_step.astype(F32))[..., None]
    lam_bar = jnp.exp(lam * step)
    b_bar = ((lam_bar - 1.0) / lam)[..., None] * lax.complex(b_re.astype(F32), b_im.astype(F32))
    c_mat = lax.complex(c_re.astype(F32), c_im.astype(F32))
    pw = [jnp.ones_like(lam_bar)]
    for _ in range(t):
        pw.append(pw[-1] * lam_bar)
    pw = jnp.stack(pw, axis=1)

    w1_f = jnp.einsum("tgp,gph->gthp", pw[0, t - 1::-1][:t], b_bar[0]).reshape(n_groups, tc, n_state)
    w1_b = jnp.einsum("tgp,gph->gthp", pw[1, :t], b_bar[1]).reshape(n_groups, tc, n_state)
    zeros = jnp.zeros((n_groups, tc, n_state), F32)

    def pair_cols(m, gi):
        return jnp.concatenate([m, zeros] if gi == 0 else [zeros, m], axis=-1)

    def w1_for(gi):
        sel = slice(gi, None, 2)
        return jnp.concatenate([
            pair_cols(jnp.real(w1_f), gi)[sel], pair_cols(jnp.imag(w1_f), gi)[sel],
            pair_cols(jnp.real(w1_b), gi)[sel], pair_cols(jnp.imag(w1_b), gi)[sel]], axis=-1)

    w1 = jnp.stack([w1_for(0), w1_for(1)], axis=1)

    k_f = jnp.real(jnp.einsum("gop,kgp,gpi->gkio", c_mat[0], pw[0, :t], b_bar[0]))
    k_b = jnp.real(jnp.einsum("gop,kgp,gpi->gkio", c_mat[1], pw[1, :t], b_bar[1]))
    s_idx = jnp.arange(t)[:, None]
    t_idx = jnp.arange(t)[None, :]
    lag_f = t_idx - s_idx
    lag_b = s_idx - t_idx
    toep_f = jnp.where((lag_f >= 0)[None, :, :, None, None], k_f[:, jnp.clip(lag_f, 0, t - 1)], 0.0)
    toep_b = jnp.where((lag_b >= 0)[None, :, :, None, None], k_b[:, jnp.clip(lag_b, 0, t - 1)], 0.0)
    eye_t = jnp.eye(t, dtype=F32)[None, :, :, None, None]
    skip = eye_t * (d_skip.astype(F32).reshape(n_groups, ch)[:, None, None, :, None]
                    * jnp.eye(ch, dtype=F32)[None, None, None])
    toep = (toep_f + toep_b + skip).transpose(0, 1, 3, 2, 4).reshape(n_groups, tc, tc)
    cf = jnp.einsum("gop,tgp->gpto", c_mat[0], pw[0, 1:t + 1]).reshape(n_groups, n_state, tc)
    cb = jnp.einsum("gop,tgp->gpto", c_mat[1], pw[1, t:0:-1]).reshape(n_groups, n_state, tc)
    zrow = jnp.zeros((n_groups, n_state, tc), F32)

    def pair_rows(m):
        even = jnp.concatenate([m, zrow], axis=1)
        odd = jnp.concatenate([zrow, m], axis=1)
        return jnp.where((jnp.arange(n_groups) % 2 == 0)[:, None, None], even, odd)

    w3 = jnp.concatenate([toep, pair_rows(jnp.real(cf)), pair_rows(-jnp.imag(cf)),
                          pair_rows(jnp.real(cb)), pair_rows(-jnp.imag(cb))], axis=1).astype(BF16)

    a_pow = pw[:, t]
    a_vec = jnp.stack([jnp.real(a_pow[0]), jnp.imag(a_pow[0]), jnp.real(a_pow[1]), jnp.imag(a_pow[1])],
                      axis=1).reshape(n_groups // 2, 2, 4, n_state)
    a_vec = a_vec.transpose(0, 2, 1, 3).reshape(n_groups // 2, 4, 2 * n_state)
    return w1, w3, a_vec


def _complex_step(a_re, a_im, h_re, h_im, s_re, s_im):
    return a_re * h_re - a_im * h_im + s_re, a_re * h_im + a_im * h_re + s_im


def _s5_scan(s_scr, h_scr, a, init, n_chunks, rows_per_chunk):
    pl_ = LANES
    if rows_per_chunk % SUBLANES == 0:
        rt = rows_per_chunk

        def body(c, carry):
            f_re, f_im, g_re, g_im = carry
            rf = pl.ds(pl.multiple_of(c * rt, SUBLANES), rt)
            rb = pl.ds(pl.multiple_of((n_chunks - 1 - c) * rt, SUBLANES), rt)
            h_scr[0, rf, :] = f_re
            h_scr[1, rf, :] = f_im
            h_scr[2, rb, :] = g_re
            h_scr[3, rb, :] = g_im
            f_re, f_im = _complex_step(a[0:1], a[1:2], f_re, f_im,
                                       s_scr[rf, 0:pl_], s_scr[rf, pl_:2 * pl_])
            g_re, g_im = _complex_step(a[2:3], a[3:4], g_re, g_im,
                                       s_scr[rb, 2 * pl_:3 * pl_], s_scr[rb, 3 * pl_:4 * pl_])
            return f_re, f_im, g_re, g_im

        return lax.fori_loop(0, n_chunks, body, tuple(init))

    assert rows_per_chunk * 2 == SUBLANES and n_chunks % 2 == 0
    half = rows_per_chunk
    n_tiles = n_chunks // 2
    low = lax.broadcasted_iota(I32, (SUBLANES, pl_), 0) < half

    def body(j, carry):
        f_re, f_im, g_re, g_im = carry
        rf = pl.ds(pl.multiple_of(j * SUBLANES, SUBLANES), SUBLANES)
        rb = pl.ds(pl.multiple_of((n_tiles - 1 - j) * SUBLANES, SUBLANES), SUBLANES)
        sf_re, sf_im = s_scr[rf, 0:pl_], s_scr[rf, pl_:2 * pl_]
        sb_re, sb_im = s_scr[rb, 2 * pl_:3 * pl_], s_scr[rb, 3 * pl_:4 * pl_]
        f1_re, f1_im = _complex_step(a[0:1], a[1:2], f_re, f_im, sf_re, sf_im)
        g1_re, g1_im = _complex_step(a[2:3], a[3:4], g_re, g_im, sb_re, sb_im)
        f1s_re, f1s_im = pltpu.roll(f1_re, half, 0), pltpu.roll(f1_im, half, 0)
        g1s_re, g1s_im = pltpu.roll(g1_re, half, 0), pltpu.roll(g1_im, half, 0)
        h_scr[0, rf, :] = jnp.where(low, f_re, f1s_re)
        h_scr[1, rf, :] = jnp.where(low, f_im, f1s_im)
        h_scr[2, rb, :] = jnp.where(low, g1s_re, g_re)
        h_scr[3, rb, :] = jnp.where(low, g1s_im, g_im)
        f2_re, f2_im = _complex_step(a[0:1], a[1:2], f1s_re, f1s_im, sf_re, sf_im)
        g2_re, g2_im = _complex_step(a[2:3], a[3:4], g1s_re, g1s_im, sb_re, sb_im)
        return (pltpu.roll(f2_re, half, 0), pltpu.roll(f2_im, half, 0),
                pltpu.roll(g2_re, half, 0), pltpu.roll(g2_im, half, 0))

    return lax.fori_loop(0, n_tiles, body, tuple(init))


def _s5_kernel(up_ref, us_ref, w1_ref, w3_ref, a_ref, h0_ref, gp_ref, gs_ref, fin_ref,
               sp_scr, ss_scr, hp_scr, hs_scr, *, geom):
    tc = up_ref.shape[-1]
    a = a_ref[0]
    w1 = [w1_ref[0, 0], w1_ref[0, 1]]
    w1_hi = [w.astype(BF16) for w in w1]

    sp_scr[...] = _dot3(up_ref[0], w1[0]) + _dot3(up_ref[1], w1[1])
    ss_scr[...] = _dot(us_ref[0].astype(BF16), w1_hi[0]) + _dot(us_ref[1].astype(BF16), w1_hi[1])

    zero = jnp.zeros((geom["bp"], LANES), F32)
    fin = _s5_scan(sp_scr, hp_scr, a, (zero, zero, zero, zero), geom["lp"] // S5_CHUNK, geom["bp"])
    for comp in range(4):
        fin_ref[0, comp] = fin[comp]
    _s5_scan(ss_scr, hs_scr, a, tuple(h0_ref[0, comp] for comp in range(4)),
             geom["ls"] // S5_CHUNK, geom["bs"])

    for u_ref, h_scr, o_ref in ((up_ref, hp_scr, gp_ref), (us_ref, hs_scr, gs_ref)):
        for gi in range(2):
            y = _dot(u_ref[gi].astype(BF16), w3_ref[gi, 0:tc, :])
            for comp in range(4):
                y += _dot(h_scr[comp].astype(BF16),
                          w3_ref[gi, tc + comp * LANES:tc + (comp + 1) * LANES, :])
            o_ref[gi] = _gelu_tanh(y).astype(o_ref.dtype)


def _s5_mixer(h, h0_re, h0_im, params, geom):
    m, d = h.shape
    t = S5_CHUNK
    ch = GROUP_CH
    n_groups = d // ch
    tc = t * ch
    bp, lp, bs, ls, mp = geom["bp"], geom["lp"], geom["bs"], geom["ls"], geom["mp"]
    rp, rs = mp // t, (m - mp) // t
    w1, w3, a_vec = _s5_operands(*params)
    n_state = a_vec.shape[-1] // 2

    def to_chunks(x, b, l):
        x = x.reshape(b, l // t, t, n_groups, ch).transpose(3, 1, 0, 2, 4)
        return x.reshape(n_groups, (l // t) * b, tc)

    def from_chunks(x, b, l):
        x = x.reshape(n_groups, l // t, b, t, ch).transpose(2, 1, 3, 0, 4)
        return x.reshape(b * l, d)

    up = to_chunks(h[:mp], bp, lp)
    us = to_chunks(h[mp:], bs, ls)
    h0 = jnp.stack([h0_re[:, 0], h0_im[:, 0], h0_re[:, 1], h0_im[:, 1]], axis=0).astype(F32)
    h0 = h0.reshape(4, bs, n_groups // 2, 2 * n_state).transpose(2, 0, 1, 3)
    reps = max(1, SUBLANES // bs)
    h0 = jnp.tile(h0, (1, 1, reps, 1))
    h0_rows = h0.shape[2]

    gp, gs, fin = pl.pallas_call(
        functools.partial(_s5_kernel, geom=geom),
        grid=(n_groups // 2,),
        in_specs=[
            pl.BlockSpec((2, rp, tc), lambda i: (i, 0, 0)),
            pl.BlockSpec((2, rs, tc), lambda i: (i, 0, 0)),
            pl.BlockSpec((1, 2, tc, 4 * LANES), lambda i: (i, 0, 0, 0)),
            pl.BlockSpec((2, tc + 4 * LANES, tc), lambda i: (i, 0, 0)),
            pl.BlockSpec((1, 4, LANES), lambda i: (i, 0, 0)),
            pl.BlockSpec((1, 4, h0_rows, LANES), lambda i: (i, 0, 0, 0)),
        ],
        out_specs=[
            pl.BlockSpec((2, rp, tc), lambda i: (i, 0, 0)),
            pl.BlockSpec((2, rs, tc), lambda i: (i, 0, 0)),
            pl.BlockSpec((1, 4, bp, LANES), lambda i: (i, 0, 0, 0)),
        ],
        out_shape=[
            jax.ShapeDtypeStruct((n_groups, rp, tc), BF16),
            jax.ShapeDtypeStruct((n_groups, rs, tc), BF16),
            jax.ShapeDtypeStruct((n_groups // 2, 4, bp, LANES), F32),
        ],
        scratch_shapes=[
            pltpu.VMEM((rp, 4 * LANES), F32),
            pltpu.VMEM((rs, 4 * LANES), F32),
            pltpu.VMEM((4, rp, LANES), F32),
            pltpu.VMEM((4, rs, LANES), F32),
        ],
        compiler_params=_cparams("arbitrary"),
        name="s5_mixer",
    )(up, us, w1, w3, a_vec, h0)

    g = jnp.concatenate([from_chunks(gp, bp, lp), from_chunks(gs, bs, ls)], axis=0)
    fin = fin.reshape(n_groups // 2, 2, 2, bp, 2, n_state).transpose(3, 1, 2, 0, 4, 5)
    fin = fin.reshape(bp, 2, 2, n_groups, n_state)
    return g, fin[:, :, 0], fin[:, :, 1]


def _softmax_with_sink(scores, sink):
    mx = sink
    for s in scores:
        mx = jnp.maximum(mx, jnp.max(s, axis=-1, keepdims=True))
    ps = [jnp.exp(s - mx) for s in scores]
    den = jnp.exp(sink - mx)
    for p in ps:
        den = den + jnp.sum(p, axis=-1, keepdims=True)
    return ps, 1.0 / den


def _attn_prompt_kernel(sink_ref, q_ref, kv_ref, o_ref, *, n_kv, q_per_kv):
    hd = HEAD_DIM
    scale = hd ** -0.5
    for kv in range(n_kv):
        k = kv_ref[:, kv * hd:(kv + 1) * hd].astype(BF16)
        v = kv_ref[:, (n_kv + kv) * hd:(n_kv + kv + 1) * hd].astype(BF16)
        for gq in range(q_per_kv):
            head = kv * q_per_kv + gq
            q = (q_ref[:, head * hd:(head + 1) * hd] * scale).astype(BF16)
            s = lax.dot_general(q, k, _NT, preferred_element_type=F32)
            (p,), inv = _softmax_with_sink([s], sink_ref[head])
            o_ref[:, head * hd:(head + 1) * hd] = (_dot(p.astype(BF16), v) * inv).astype(o_ref.dtype)


def _attn_prompt(qkv, sink, geom, n_kv):
    d = geom["d"]
    bp, lp = geom["bp"], geom["lp"]
    kvw = 2 * n_kv * HEAD_DIM
    n_heads = d // HEAD_DIM
    return pl.pallas_call(
        functools.partial(_attn_prompt_kernel, n_kv=n_kv, q_per_kv=n_heads // n_kv),
        grid=(bp,),
        in_specs=[
            pl.BlockSpec(memory_space=pltpu.SMEM),
            pl.BlockSpec((lp, d), lambda b: (b, 0)),
            pl.BlockSpec((lp, kvw), lambda b: (b, d // kvw)),
        ],
        out_specs=pl.BlockSpec((lp, d), lambda b: (b, 0)),
        out_shape=jax.ShapeDtypeStruct((bp * lp, d), BF16),
        compiler_params=_cparams("arbitrary"),
        name="attn_prompt",
    )(sink, qkv, qkv)


def _rope_tables(length):
    pairs = HEAD_DIM // 4
    pos = jnp.arange(length)
    row = (pos // GRID_W).astype(F32)
    col = (pos % GRID_W).astype(F32)
    inv_freq = ROPE_BASE ** (-jnp.arange(pairs, dtype=F32) / pairs)
    lane = jnp.arange(LANES) % HEAD_DIM
    axis_pos = jnp.where((lane // (2 * pairs))[None, :] == 0, row[:, None], col[:, None])
    ang = axis_pos * inv_freq[lane % pairs][None, :]
    sign = jnp.where((lane % (2 * pairs)) < pairs, -1.0, 1.0)[None, :]
    return jnp.cos(ang), jnp.sin(ang) * sign


def _rope(x, cos, sin):
    w = x.shape[1]
    pairs = HEAD_DIM // 4
    reps = w // LANES
    cos_w = jnp.concatenate([cos] * reps, axis=1) if reps > 1 else cos
    sin_w = jnp.concatenate([sin] * reps, axis=1) if reps > 1 else sin
    lane = lax.broadcasted_iota(I32, x.shape, 1)
    partner = jnp.where(lane % (2 * pairs) < pairs, pltpu.roll(x, w - pairs, 1), pltpu.roll(x, pairs, 1))
    return x * cos_w + partner * sin_w


def _attn_latent_kernel(sink_ref, q_ref, kvp_ref, kvo_ref, kvn_ref, ck_ref, cv_ref,
                        cos_p, sin_p, cos_o, sin_o, cos_n, sin_n, o_ref, *, n_kv, q_per_kv, n_blocks):
    hd = HEAD_DIM
    blk = ATTN_BLOCK
    scale = hd ** -0.5
    i = pl.program_id(1)
    kw = n_kv * hd
    k_win = jnp.concatenate([
        _rope(kvp_ref[:, 0:kw], cos_p[...], sin_p[...]),
        _rope(kvo_ref[:, 0:kw], cos_o[...], sin_o[...]),
        _rope(kvn_ref[:, 0:kw], cos_n[...], sin_n[...])], axis=0).astype(BF16)
    v_win = jnp.concatenate([kvp_ref[:, kw:2 * kw], kvo_ref[:, kw:2 * kw], kvn_ref[:, kw:2 * kw]],
                            axis=0).astype(BF16)
    rows = q_per_kv * blk
    r_in_blk = lax.broadcasted_iota(I32, (rows, 3 * blk), 0) % blk
    c_idx = lax.broadcasted_iota(I32, (rows, 3 * blk), 1)
    k_pos = (i - 1) * blk + c_idx
    mask = (jnp.abs(c_idx - blk - r_in_blk) <= WINDOW) & (k_pos >= 0) & (k_pos < n_blocks * blk)
    for kv in range(n_kv):
        qg = _rope(q_ref[:, kv * q_per_kv * hd:(kv + 1) * q_per_kv * hd], cos_o[...], sin_o[...]) * scale
        q = jnp.concatenate([qg[:, gq * hd:(gq + 1) * hd] for gq in range(q_per_kv)], axis=0).astype(BF16)
        sink = jnp.concatenate(
            [jnp.full((blk, 1), sink_ref[kv * q_per_kv + gq], F32) for gq in range(q_per_kv)], axis=0)
        s_ctx = lax.dot_general(q, ck_ref[0, :, kv * hd:(kv + 1) * hd].astype(BF16), _NT,
                                preferred_element_type=F32)
        s_win = lax.dot_general(q, k_win[:, kv * hd:(kv + 1) * hd], _NT, preferred_element_type=F32)
        s_win = jnp.where(mask, s_win, NEG_INF)
        (p_ctx, p_win), inv = _softmax_with_sink([s_ctx, s_win], sink)
        o = (_dot(p_ctx.astype(BF16), cv_ref[0, :, kv * hd:(kv + 1) * hd].astype(BF16))
             + _dot(p_win.astype(BF16), v_win[:, kv * hd:(kv + 1) * hd])) * inv
        for gq in range(q_per_kv):
            head = kv * q_per_kv + gq
            o_ref[:, head * hd:(head + 1) * hd] = o[gq * blk:(gq + 1) * blk].astype(o_ref.dtype)


def _attn_latent(qkv, ctx_k, ctx_v, sink, geom, n_kv):
    d = geom["d"]
    bs, ls, mp = geom["bs"], geom["ls"], geom["mp"]
    blk = ATTN_BLOCK
    nb = ls // blk
    kvw = 2 * n_kv * HEAD_DIM
    n_heads = d // HEAD_DIM
    past = ctx_k.shape[1]
    ck = ctx_k.reshape(bs, past, n_kv * HEAD_DIM).astype(F32)
    cv = ctx_v.reshape(bs, past, n_kv * HEAD_DIM).astype(F32)
    cos, sin = _rope_tables(ls)
    base = mp // blk

    def prev(i):
        return jnp.maximum(i - 1, 0)

    def nxt(i):
        return jnp.minimum(i + 1, nb - 1)

    def kv_spec(f):
        return pl.BlockSpec((blk, kvw), lambda b, i: (base + b * nb + f(i), d // kvw))

    def tab_spec(f):
        return pl.BlockSpec((blk, LANES), lambda b, i: (f(i), 0))

    same = lambda i: i
    return pl.pallas_call(
        functools.partial(_attn_latent_kernel, n_kv=n_kv, q_per_kv=n_heads // n_kv, n_blocks=nb),
        grid=(bs, nb),
        in_specs=[
            pl.BlockSpec(memory_space=pltpu.SMEM),
            pl.BlockSpec((blk, d), lambda b, i: (base + b * nb + i, 0)),
            kv_spec(prev), kv_spec(same), kv_spec(nxt),
            pl.BlockSpec((1, past, n_kv * HEAD_DIM), lambda b, i: (b, 0, 0)),
            pl.BlockSpec((1, past, n_kv * HEAD_DIM), lambda b, i: (b, 0, 0)),
            tab_spec(prev), tab_spec(prev), tab_spec(same), tab_spec(same), tab_spec(nxt), tab_spec(nxt),
        ],
        out_specs=pl.BlockSpec((blk, d), lambda b, i: (b * nb + i, 0)),
        out_shape=jax.ShapeDtypeStruct((bs * ls, d), BF16),
        compiler_params=_cparams("arbitrary", "arbitrary"),
        name="attn_latent",
    )(sink, qkv, qkv, qkv, qkv, ck, cv, cos, sin, cos, sin, cos, sin)


def _pack_bf16_pairs(x):
    n = x.shape[1] // 2
    xb = x.astype(BF16).astype(F32)
    hi = lax.bitcast_convert_type(xb[:, :n], U32)
    lo = lax.bitcast_convert_type(xb[:, n:], U32)
    return hi | (lo >> 16)


def _store_row_tiles(ref, words):
    for s in range(ref.shape[1]):
        ref[:, s, :] = words[:, s * LANES:(s + 1) * LANES]


def _unpack_hi(words):
    return lax.bitcast_convert_type(words & jnp.uint32(0xFFFF0000), F32)


def _unpack_lo(words):
    return lax.bitcast_convert_type(words << 16, F32)


def _load_row_tiles_bf16(ref, ns):
    rows = ref.shape[0] // ns
    tiles = [ref[pl.ds(s, rows, stride=ns), :] for s in range(ns)]
    return jnp.concatenate([_unpack_hi(t).astype(BF16) for t in tiles]
                           + [_unpack_lo(t).astype(BF16) for t in tiles], axis=1)


def _router_kernel(x_ref, g_ref, sc_ref, sh_ref, wt_hi_ref, wt_lo_ref, bias_ref,
                   h3_ref, lposd_ref, lposp_ref, w_ref, win_ref, cnt_ref, base_ref, base_scr, base_col_scr,
                   *, n_experts):
    step = pl.program_id(0)

    @pl.when(step == 0)
    def _():
        base_scr[...] = jnp.zeros_like(base_scr)
        base_col_scr[...] = jnp.zeros_like(base_col_scr)

    h = _norm_mod(x_ref[...], g_ref[...], sc_ref[0], sh_ref[0])
    _store_row_tiles(h3_ref, _pack_bf16_pairs(h))
    tm = h.shape[0]
    h_hi, h_lo = _split(h)
    wt_hi = wt_hi_ref[...]
    logits = (lax.dot_general(wt_hi, h_hi, _NT, preferred_element_type=F32)
              + lax.dot_general(wt_lo_ref[...], h_hi, _NT, preferred_element_type=F32)
              + lax.dot_general(wt_hi, h_lo, _NT, preferred_element_type=F32))
    scores = _sigmoid(logits)
    sel = scores + bias_ref[:, 0:1]
    per_group = n_experts // N_EXPERT_GROUPS
    e_iota = lax.broadcasted_iota(I32, (n_experts, tm), 0)
    big = jnp.int32(1 << 30)

    grp = []
    for gidx in range(N_EXPERT_GROUPS):
        v = sel[gidx * per_group:(gidx + 1) * per_group]
        r = lax.broadcasted_iota(I32, v.shape, 0) + gidx * per_group
        m1 = jnp.max(v, axis=0, keepdims=True)
        i1 = jnp.min(jnp.where(v == m1, r, big), axis=0, keepdims=True)
        m2 = jnp.max(jnp.where(r == i1, -jnp.inf, v), axis=0, keepdims=True)
        grp.append(jnp.broadcast_to(m1 + m2, v.shape))
    cur = jnp.concatenate(grp, axis=0)
    g_iota = e_iota // per_group
    cand = jnp.full(sel.shape, -jnp.inf, F32)
    for _ in range(TOPK_GROUPS):
        mx = jnp.max(cur, axis=0, keepdims=True)
        gi = jnp.min(jnp.where(cur == mx, g_iota, big), axis=0, keepdims=True)
        hit = g_iota == gi
        cand = jnp.where(hit, sel, cand)
        cur = jnp.where(hit, -jnp.inf, cur)

    idxs, wts = [], []
    onehot = jnp.zeros((n_experts, tm), F32)
    for _ in range(TOP_K):
        mx = jnp.max(cand, axis=0, keepdims=True)
        ei = jnp.min(jnp.where(cand == mx, e_iota, big), axis=0, keepdims=True)
        hit = e_iota == ei
        idxs.append(ei)
        wts.append(jnp.sum(jnp.where(hit, scores, 0.0), axis=0, keepdims=True))
        onehot = jnp.where(hit, 1.0, onehot)
        cand = jnp.where(hit, -jnp.inf, cand)
    wsum = wts[0]
    for wk in wts[1:]:
        wsum = wsum + wk
    w_ref[0] = jnp.concatenate(wts, axis=0) / wsum * ROUTED_SCALE

    onehot_b = onehot.astype(BF16)
    t_row = lax.broadcasted_iota(I32, (tm, tm), 0)
    t_col = lax.broadcasted_iota(I32, (tm, tm), 1)
    earlier_tok = _dot(onehot_b, jnp.where(t_row < t_col, 1.0, 0.0).astype(BF16))
    cnt = jnp.sum(onehot, axis=1, keepdims=True)
    cnt_pad = jnp.floor((cnt + (DMA_WINDOW - 1)) * (1.0 / DMA_WINDOW)) * DMA_WINDOW
    e_row = lax.broadcasted_iota(I32, (n_experts, n_experts), 0)
    e_col = lax.broadcasted_iota(I32, (n_experts, n_experts), 1)
    lower = jnp.where(e_col < e_row, 1.0, 0.0).astype(BF16)
    start_d = _dot(lower, jnp.broadcast_to(cnt, (n_experts, LANES)).astype(BF16))[:, 0:1]
    start_p = _dot(lower, jnp.broadcast_to(cnt_pad, (n_experts, LANES)).astype(BF16))[:, 0:1]
    pos_d = start_d + earlier_tok
    pos_p = start_p + earlier_tok
    lposd_ref[0] = jnp.concatenate(
        [jnp.sum(jnp.where(e_iota == ei, pos_d, 0.0), axis=0, keepdims=True) for ei in idxs], axis=0).astype(I32)
    lposp_ref[0] = jnp.concatenate(
        [jnp.sum(jnp.where(e_iota == ei, pos_p, 0.0), axis=0, keepdims=True) for ei in idxs], axis=0).astype(I32)

    n_win_max = win_ref.shape[2]
    nw = cnt_pad * (1.0 / DMA_WINDOW)
    w_start = _dot(lower, jnp.broadcast_to(nw, (n_experts, LANES)).astype(BF16))[:, 0:1]
    q = lax.broadcasted_iota(I32, (n_experts, n_win_max), 1).astype(F32)
    w_expert = jnp.sum(jnp.where(w_start + nw <= q, 1.0, 0.0), axis=0, keepdims=True)
    mine = lax.broadcasted_iota(I32, (n_experts, n_win_max), 0).astype(F32) == w_expert

    def pick(col):
        return jnp.sum(jnp.where(mine, col, 0.0), axis=0, keepdims=True)

    q_row = lax.broadcasted_iota(I32, (1, n_win_max), 1).astype(F32)
    offset = (q_row - pick(w_start)) * DMA_WINDOW
    total = jnp.broadcast_to(jnp.sum(nw, axis=0, keepdims=True), (1, n_win_max))
    zero_row = jnp.zeros((1, n_win_max), F32)
    win_ref[0] = jnp.concatenate(
        [w_expert, pick(base_col_scr[:, 0:1]) + offset, pick(start_d) + offset, pick(start_p) + offset, total,
         zero_row, zero_row, zero_row], axis=0).astype(I32)
    base_col_scr[...] = base_col_scr[...] + cnt

    cnt_row = lax.dot_general(jnp.ones((SUBLANES, tm), BF16), onehot_b, _NT, preferred_element_type=F32)
    cnt_ref[0] = cnt_row
    base_ref[0] = base_scr[...]
    base_scr[...] = base_scr[...] + cnt_row


def _route(x, g, sc, sh, w_router, bias, geom):
    m, d = x.shape
    n_experts = w_router.shape[1]
    tm = MOE_TILE
    n_tiles = m // tm
    ns = d // (2 * LANES)
    wt = w_router.astype(F32).T
    wt_hi = wt.astype(BF16)
    wt_lo = (wt - wt_hi.astype(F32)).astype(BF16)
    bias_b = jnp.broadcast_to(bias.astype(F32)[:, None], (n_experts, LANES))
    midx = functools.partial(_mod_index, n_prompt_tiles=geom["mp"] // tm,
                             tiles_per_latent_seq=geom["ls"] // tm)
    n_win_max = TOP_K * tm // DMA_WINDOW + n_experts
    tok_spec = pl.BlockSpec((1, TOP_K, tm), lambda i: (i, 0, 0))
    win_spec = pl.BlockSpec((1, SUBLANES, n_win_max), lambda i: (i, 0, 0))
    exp_spec = pl.BlockSpec((1, SUBLANES, n_experts), lambda i: (i, 0, 0))
    return pl.pallas_call(
        functools.partial(_router_kernel, n_experts=n_experts),
        grid=(n_tiles,),
        in_specs=[
            pl.BlockSpec((tm, d), lambda i: (i, 0)),
            pl.BlockSpec((1, d), lambda i: (0, 0)),
            pl.BlockSpec((1, 1, d), lambda i: (midx(i), 0, 0)),
            pl.BlockSpec((1, 1, d), lambda i: (midx(i), 0, 0)),
            pl.BlockSpec((n_experts, d), lambda i: (0, 0)),
            pl.BlockSpec((n_experts, d), lambda i: (0, 0)),
            pl.BlockSpec((n_experts, LANES), lambda i: (0, 0)),
        ],
        out_specs=[
            pl.BlockSpec((tm, ns, LANES), lambda i: (i, 0, 0)),
            tok_spec, tok_spec, tok_spec, win_spec, exp_spec, exp_spec,
        ],
        out_shape=[
            jax.ShapeDtypeStruct((m, ns, LANES), U32),
            jax.ShapeDtypeStruct((n_tiles, TOP_K, tm), I32),
            jax.ShapeDtypeStruct((n_tiles, TOP_K, tm), I32),
            jax.ShapeDtypeStruct((n_tiles, TOP_K, tm), F32),
            jax.ShapeDtypeStruct((n_tiles, SUBLANES, n_win_max), I32),
            jax.ShapeDtypeStruct((n_tiles, SUBLANES, n_experts), F32),
            jax.ShapeDtypeStruct((n_tiles, SUBLANES, n_experts), F32),
        ],
        scratch_shapes=[pltpu.VMEM((SUBLANES, n_experts), F32), pltpu.VMEM((n_experts, LANES), F32)],
        compiler_params=_cparams("arbitrary"),
        name="moe_router",
    )(x, g.reshape(1, d), sc, sh, wt_hi, wt_lo, bias_b)


_FILL_SIZES = tuple(1 << b for b in range(EXPERT_TILE.bit_length() - 1, -1, -1))


def _dispatch_kernel(fill_start_ref, fill_n_ref, pad_start_ref, lpos_hbm, win_hbm, h3_ref, xs_hbm,
                     lpos_smem, win_smem, buf, zero_scr, idx_sem, row_sem, *, n_experts):
    step = pl.program_id(0)
    tm = h3_ref.shape[0]
    n_assign = TOP_K * tm
    win = DMA_WINDOW

    @pl.when(step == 0)
    def _():
        zero_scr[...] = jnp.zeros_like(zero_scr)
        buf[...] = jnp.zeros_like(buf)

        def fill_copies(e):
            start = fill_start_ref[e]
            n = fill_n_ref[e]
            off = jnp.int32(0)
            out = []
            for size in _FILL_SIZES:
                take = (n & size) != 0
                out.append((take, pltpu.make_async_copy(zero_scr.at[pl.ds(0, size)],
                                                        xs_hbm.at[pl.ds(start + off, size)], row_sem)))
                off = off + jnp.where(take, size, 0)
            return out

        def issue(e, carry):
            for take, cp in fill_copies(e):
                @pl.when(take)
                def _():
                    cp.start()
            return carry

        def drain(e, carry):
            for take, cp in fill_copies(e):
                @pl.when(take)
                def _():
                    cp.wait()
            return carry

        lax.fori_loop(0, n_experts, issue, 0)
        lax.fori_loop(0, n_experts, drain, 0)

    @pl.when(step > 0)
    def _():
        tile = step - 1
        idx_copies = [pltpu.make_async_copy(lpos_hbm.at[tile], lpos_smem, idx_sem.at[0]),
                      pltpu.make_async_copy(win_hbm.at[tile], win_smem, idx_sem.at[1])]
        for cp in idx_copies:
            cp.start()
        for cp in idx_copies:
            cp.wait()

        def per_token(t, carry):
            row = h3_ref[t]
            for k in range(TOP_K):
                buf[lpos_smem[k * tm + t]] = row
            return carry

        lax.fori_loop(0, tm, per_token, 0, unroll=2)

        def window_copy(src, dst):
            return pltpu.make_async_copy(buf.at[pl.ds(src, win)], xs_hbm.at[pl.ds(dst, win)], row_sem)

        n_win_max = win_smem.shape[0] // SUBLANES
        n_win = win_smem[4 * n_win_max]

        def issue(q, carry):
            dst = pad_start_ref[win_smem[q]] + win_smem[n_win_max + q]
            window_copy(win_smem[2 * n_win_max + q], dst).start()
            return carry

        lax.fori_loop(0, n_win, issue, 0)

        def drain(j, carry):
            window_copy(0, 0).wait()
            return carry

        lax.fori_loop(0, n_win, drain, 0)


def _dispatch(h3, lposd, wins, fill_start, fill_n, pad_start, n_slots):
    m, ns, _ = h3.shape
    tm = MOE_TILE
    n_tiles = m // tm
    n_experts = fill_start.shape[0]
    win_words = wins.shape[1] * wins.shape[2]

    return pl.pallas_call(
        functools.partial(_dispatch_kernel, n_experts=n_experts),
        grid_spec=pltpu.PrefetchScalarGridSpec(
            num_scalar_prefetch=3,
            grid=(n_tiles + 1,),
            in_specs=[
                pl.BlockSpec(memory_space=pl.ANY),
                pl.BlockSpec(memory_space=pl.ANY),
                pl.BlockSpec((tm, ns, LANES), lambda i, fs, fn, ps: (jnp.maximum(i - 1, 0), 0, 0)),
            ],
            out_specs=pl.BlockSpec(memory_space=pl.ANY),
            scratch_shapes=[
                pltpu.SMEM((TOP_K * tm,), I32),
                pltpu.SMEM((win_words,), I32),
                pltpu.VMEM((TOP_K * tm + DMA_WINDOW, ns, LANES), U32),
                pltpu.VMEM((EXPERT_TILE, ns, LANES), U32),
                pltpu.SemaphoreType.DMA((2,)),
                pltpu.SemaphoreType.DMA,
            ],
        ),
        out_shape=jax.ShapeDtypeStruct((n_slots, ns, LANES), U32),
        compiler_params=_cparams("arbitrary"),
        name="moe_dispatch",
    )(fill_start, fill_n, pad_start, lposd.reshape(n_tiles, TOP_K * tm), wins.reshape(n_tiles, win_words), h3)


def _expert_kernel(be_ref, first_ref, nxt_ref, par_ref, na_ref, xs_ref, wg_hbm, wu_hbm, wd_hbm, y_ref,
                   wg_buf, wu_buf, wd_buf, wg_scr, wu_scr, wd_scr, sem, *, layer):
    i = pl.program_id(0)

    def fetch(expert, slot):
        return [pltpu.make_async_copy(src.at[layer, expert], dst.at[slot], sem.at[slot, j])
                for j, (src, dst) in enumerate(((wg_hbm, wg_buf), (wu_hbm, wu_buf), (wd_hbm, wd_buf)))]

    @pl.when(i < na_ref[0])
    def _():
        slot = par_ref[i]

        @pl.when(i == 0)
        def _():
            for cp in fetch(be_ref[0], slot):
                cp.start()

        @pl.when(first_ref[i] == 1)
        def _():
            @pl.when(nxt_ref[i] >= 0)
            def _():
                for cp in fetch(nxt_ref[i], 1 - slot):
                    cp.start()

            for cp in fetch(be_ref[i], slot):
                cp.wait()
            wg_scr[...] = wg_buf[slot].astype(BF16)
            wu_scr[...] = wu_buf[slot].astype(BF16)
            wd_scr[...] = wd_buf[slot].astype(BF16)

        x = _load_row_tiles_bf16(xs_ref, y_ref.shape[1])
        gate = _dot(x, wg_scr[...])
        up = _dot(x, wu_scr[...])
        act = (gate * _sigmoid(gate) * up).astype(BF16)
        _store_row_tiles(y_ref, _pack_bf16_pairs(_dot(act, wd_scr[...])))


def _experts(xs3, sched, w_gate, w_up, w_down, layer):
    n_slots, ns, _ = xs3.shape
    tm = EXPERT_TILE
    d, de = w_gate.shape[-2:]
    n_blocks = n_slots // tm

    def blk(i, be, first, nxt, par, na):
        return (jnp.minimum(i, na[0] - 1), 0, 0)

    return pl.pallas_call(
        functools.partial(_expert_kernel, layer=layer),
        grid_spec=pltpu.PrefetchScalarGridSpec(
            num_scalar_prefetch=5,
            grid=(n_blocks,),
            in_specs=[
                pl.BlockSpec((tm * ns, LANES), lambda *a: blk(*a)[:2]),
                pl.BlockSpec(memory_space=pl.ANY),
                pl.BlockSpec(memory_space=pl.ANY),
                pl.BlockSpec(memory_space=pl.ANY),
            ],
            out_specs=pl.BlockSpec((tm, ns, LANES), blk),
            scratch_shapes=[
                pltpu.VMEM((2, d, de), F32),
                pltpu.VMEM((2, d, de), F32),
                pltpu.VMEM((2, de, d), F32),
                pltpu.VMEM((d, de), BF16),
                pltpu.VMEM((d, de), BF16),
                pltpu.VMEM((de, d), BF16),
                pltpu.SemaphoreType.DMA((2, 3)),
            ],
        ),
        out_shape=jax.ShapeDtypeStruct((n_slots, ns, LANES), U32),
        compiler_params=_cparams("arbitrary"),
        name="moe_experts",
    )(*sched, xs3.reshape(n_slots * ns, LANES), w_gate, w_up, w_down)


def _combine_kernel(pad_start_ref, lpos_hbm, w_hbm, win_hbm, y_hbm, x_ref, h3_ref, g_ref, sg_ref, su_ref, sd_ref,
                    o_ref, lpos_smem, w_smem, win_smem, ybuf, rout, idx_sem, row_sem):
    step = pl.program_id(0)
    tm = x_ref.shape[0]
    ns = ybuf.shape[1]
    win = DMA_WINDOW
    idx_copies = [pltpu.make_async_copy(lpos_hbm.at[step], lpos_smem, idx_sem.at[0]),
                  pltpu.make_async_copy(w_hbm.at[step], w_smem, idx_sem.at[1]),
                  pltpu.make_async_copy(win_hbm.at[step], win_smem, idx_sem.at[2])]
    for cp in idx_copies:
        cp.start()

    @pl.when(step == 0)
    def _():
        ybuf[...] = jnp.zeros_like(ybuf)

    for cp in idx_copies:
        cp.wait()

    def window_copy(src, dst):
        return pltpu.make_async_copy(y_hbm.at[pl.ds(src, win)], ybuf.at[pl.ds(dst, win)], row_sem)

    n_win_max = win_smem.shape[0] // SUBLANES
    n_win = win_smem[4 * n_win_max]

    def issue(q, carry):
        src = pad_start_ref[win_smem[q]] + win_smem[n_win_max + q]
        window_copy(src, win_smem[3 * n_win_max + q]).start()
        return carry

    lax.fori_loop(0, n_win, issue, 0)

    hb = _load_row_tiles_bf16(h3_ref, ns)
    gate = _dot(hb, sg_ref[...])
    up = _dot(hb, su_ref[...])
    shared = _dot((gate * _sigmoid(gate) * up).astype(BF16), sd_ref[...])

    def drain(j, carry):
        window_copy(0, 0).wait()
        return carry

    lax.fori_loop(0, n_win, drain, 0)

    def per_token(t, carry):
        acc_hi = jnp.zeros((ns, LANES), F32)
        acc_lo = jnp.zeros((ns, LANES), F32)
        for k in range(TOP_K):
            words = ybuf[lpos_smem[k * tm + t]]
            wk = w_smem[k * tm + t]
            acc_hi = acc_hi + wk * _unpack_hi(words)
            acc_lo = acc_lo + wk * _unpack_lo(words)
        rout[pl.ds(pl.multiple_of(t * 2 * ns, ns), ns), :] = acc_hi
        rout[pl.ds(pl.multiple_of(t * 2 * ns + ns, ns), ns), :] = acc_lo
        return carry

    lax.fori_loop(0, tm, per_token, 0, unroll=2)
    routed = jnp.concatenate([rout[pl.ds(s, tm, stride=2 * ns), :] for s in range(2 * ns)], axis=1)
    o_ref[...] = x_ref[...] + g_ref[0] * (routed + shared)


def _combine(lposp, w_sel, wins, pad_start, y3, x, h3, gate, s_gate, s_up, s_down, geom, layer):
    m, d = x.shape
    tm = MOE_TILE
    n_tiles = m // tm
    de = s_gate.shape[-1]
    ns = y3.shape[1]
    n_experts = pad_start.shape[0]
    n_assign = TOP_K * tm
    win_words = wins.shape[1] * wins.shape[2]
    midx = functools.partial(_mod_index, n_prompt_tiles=geom["mp"] // tm,
                             tiles_per_latent_seq=geom["ls"] // tm)
    return pl.pallas_call(
        _combine_kernel,
        grid_spec=pltpu.PrefetchScalarGridSpec(
            num_scalar_prefetch=1,
            grid=(n_tiles,),
            in_specs=[
                pl.BlockSpec(memory_space=pl.ANY),
                pl.BlockSpec(memory_space=pl.ANY),
                pl.BlockSpec(memory_space=pl.ANY),
                pl.BlockSpec(memory_space=pl.ANY),
                pl.BlockSpec((tm, d), lambda i, ps: (i, 0)),
                pl.BlockSpec((tm * ns, LANES), lambda i, ps: (i, 0)),
                pl.BlockSpec((1, 1, d), lambda i, ps: (midx(i), 0, 0)),
                pl.BlockSpec((None, d, de), lambda i, ps: (layer, 0, 0)),
                pl.BlockSpec((None, d, de), lambda i, ps: (layer, 0, 0)),
                pl.BlockSpec((None, de, d), lambda i, ps: (layer, 0, 0)),
            ],
            out_specs=pl.BlockSpec((tm, d), lambda i, ps: (i, 0)),
            scratch_shapes=[
                pltpu.SMEM((n_assign,), I32),
                pltpu.SMEM((n_assign,), F32),
                pltpu.SMEM((win_words,), I32),
                pltpu.VMEM((n_assign + n_experts * (DMA_WINDOW - 1), ns, LANES), U32),
                pltpu.VMEM((tm * 2 * ns, LANES), F32),
                pltpu.SemaphoreType.DMA((3,)),
                pltpu.SemaphoreType.DMA,
            ],
        ),
        out_shape=jax.ShapeDtypeStruct((m, d), F32),
        compiler_params=_cparams("arbitrary"),
        name="moe_combine",
    )(pad_start, lposp.reshape(n_tiles, n_assign), w_sel.reshape(n_tiles, n_assign),
      wins.reshape(n_tiles, win_words), y3, x, h3.reshape(m * ns, LANES), gate, s_gate, s_up, s_down)


def _moe_layer(x, norm_g, sc, sh, gate, w_router, bias, w_gate, w_up, w_down, s_gate, s_up, s_down,
               geom, layer):
    m, d = x.shape
    n_experts = w_router.shape[1]
    tm = EXPERT_TILE
    spare = DMA_WINDOW - 1
    h3, lposd, lposp, w_sel, wins, tile_cnt, tile_base = _route(x, norm_g, sc, sh, w_router, bias, geom)

    counts = (tile_base[-1, 0] + tile_cnt[-1, 0]).astype(I32)
    padded = jnp.where(counts > 0, (counts + spare + tm - 1) // tm * tm, 0)
    pad_end = jnp.cumsum(padded)
    pad_start = pad_end - padded
    n_blocks = -(-(m * TOP_K + n_experts * (spare + tm - 1)) // tm)
    n_active = pad_end[-1] // tm
    blocks = jnp.arange(n_blocks, dtype=I32)
    block_expert = jnp.minimum(jnp.searchsorted(pad_end, blocks * tm, side="right"), n_experts - 1).astype(I32)
    prev_expert = jnp.concatenate([jnp.full((1,), -1, I32), block_expert[:-1]])
    first = (block_expert != prev_expert).astype(I32)
    parity = (jnp.cumsum(first) - 1) % 2
    next_block = pad_end[block_expert] // tm
    next_expert = jnp.where(next_block < n_active, block_expert[jnp.minimum(next_block, n_blocks - 1)], -1)
    sched = (block_expert, first, next_expert.astype(I32), parity.astype(I32), n_active.astype(I32).reshape(1))
    pad_start = pad_start.astype(I32)

    xs3 = _dispatch(h3, lposd, wins, pad_start + counts, padded - counts, pad_start, n_blocks * tm)
    y3 = _experts(xs3, sched, w_gate, w_up, w_down, layer)
    return _combine(lposp, w_sel, wins, pad_start, y3, x, h3, gate, s_gate.astype(BF16), s_up.astype(BF16),
                    s_down.astype(BF16), geom, layer)


def kernel(x_prompt, x_sample, c, state_ssm_re, state_ssm_im, cache_k, cache_v, c_ctx, ada_w, ada_b, norm1_g, norm2_g, final_norm_g, s5_lam_re, s5_lam_im, s5_log_step, s5_b_re, s5_b_im, s5_c_re, s5_c_im, s5_d, s5_w_glu, attn_w_qkv, attn_b_qkv, attn_w_o, attn_sink, moe_w_router, moe_router_bias, moe_w_gate, moe_w_up, moe_w_down, moe_shared_gate, moe_shared_up, moe_shared_down):
    bp, lp, d = x_prompt.shape
    bs, ls, _ = x_sample.shape
    depth = ada_w.shape[0]
    n_kv = cache_k.shape[3]
    mp, ms = bp * lp, bs * ls
    geom = dict(bp=bp, lp=lp, bs=bs, ls=ls, mp=mp, d=d)
    assert lp % ROW_TILE == 0 and ls % ROW_TILE == 0 and bs in (4,) and bp % SUBLANES == 0
    assert (2 * n_kv * HEAD_DIM) <= d and d % (2 * n_kv * HEAD_DIM) == 0

    x = jnp.concatenate([x_prompt.reshape(mp, d), x_sample.reshape(ms, d)], axis=0).astype(F32)

    n_mod = -(-(bs + 1) // SUBLANES) * SUBLANES
    cvec = jnp.zeros((n_mod, d), F32).at[0].set(c_ctx.astype(F32)).at[1:bs + 1].set(c.astype(F32))
    mods = _ada_params(cvec, ada_w, ada_b).reshape(depth, n_mod, 6, 1, d)

    new_re, new_im, new_k, new_v = [], [], [], []
    for i in range(depth):
        j = i // N_MIXERS
        sh1, sc1, g1, sh2, sc2, g2 = (mods[i, :, part] for part in range(6))
        if i % N_MIXERS == 0:
            h = _norm_mod_call(x, norm1_g[i], sc1, sh1, geom, F32)
            params = (s5_lam_re[j], s5_lam_im[j], s5_log_step[j], s5_b_re[j], s5_b_im[j],
                      s5_c_re[j], s5_c_im[j], s5_d[j])
            g, st_re, st_im = _s5_mixer(h, state_ssm_re[:, j], state_ssm_im[:, j], params, geom)
            new_re.append(st_re)
            new_im.append(st_im)
            x = _mm_resid(g, s5_w_glu[j].astype(BF16), x, g1, geom, glu=True)
        else:
            h = _norm_mod_call(x, norm1_g[i], sc1, sh1, geom, BF16)
            qkv = _mm_bias(h, attn_w_qkv[j].astype(BF16), attn_b_qkv[j].astype(F32))
            sink = attn_sink[j].astype(F32)
            o = jnp.concatenate([
                _attn_prompt(qkv, sink, geom, n_kv),
                _attn_latent(qkv, cache_k[:, j], cache_v[:, j], sink, geom, n_kv)], axis=0)
            kw = n_kv * HEAD_DIM
            new_k.append(qkv[:mp, d:d + kw].reshape(bp, lp, n_kv, HEAD_DIM))
            new_v.append(qkv[:mp, d + kw:d + 2 * kw].reshape(bp, lp, n_kv, HEAD_DIM))
            x = _mm_resid(o, attn_w_o[j].astype(BF16), x, g1, geom, glu=False)
        x = _moe_layer(x, norm2_g[i], sc2, sh2, g2, moe_w_router[i], moe_router_bias[i],
                       moe_w_gate, moe_w_up, moe_w_down, moe_shared_gate, moe_shared_up, moe_shared_down,
                       geom, i)

    zeros = jnp.zeros((n_mod, 1, d), F32)
    y = _norm_mod_call(x, final_norm_g, zeros, zeros, geom, F32)
    return (y[:mp].reshape(bp, lp, d), y[mp:].reshape(bs, ls, d),
            jnp.stack(new_re, axis=1), jnp.stack(new_im, axis=1),
            jnp.stack(new_k, axis=1), jnp.stack(new_v, axis=1))
```

```python
import functools
import math

import jax
import jax.numpy as jnp
from jax import lax
from jax.experimental import pallas as pl
from jax.experimental.pallas import tpu as pltpu

F32 = jnp.float32
BF16 = jnp.bfloat16
I32 = jnp.int32

GROUP_CH = 16
HEAD_DIM = 64
WINDOW = 128
ATTN_BLOCK = 128
GRID_W = 64
ROPE_BASE = 10000.0
TOP_K = 8
N_EXPERT_GROUPS = 8
TOPK_GROUPS = 4
ROUTED_SCALE = 2.5
RMS_EPS = 1e-6
NEG_INF = -1e30
N_MIXERS = 2

LANES = 128
SUBLANES = 8
VMEM_LIMIT_BYTES = 56 * 1024 * 1024

S5_CHUNK = 16
ROW_TILE = 256
EXPERT_TILE = 256
MOE_TILE = 256
DMA_WINDOW = 8
U32 = jnp.uint32

_NT = (((1,), (1,)), ((), ()))


def _cparams(*sem):
    return pltpu.CompilerParams(dimension_semantics=sem, vmem_limit_bytes=VMEM_LIMIT_BYTES)


def _dot(a, b):
    return jnp.dot(a, b, preferred_element_type=F32)


def _split(x):
    hi = x.astype(BF16)
    lo = (x - hi.astype(F32)).astype(BF16)
    return hi, lo


def _dot3(a, b):
    a_hi, a_lo = _split(a)
    b_hi, b_lo = _split(b)
    return _dot(a_hi, b_hi) + _dot(a_lo, b_hi) + _dot(a_hi, b_lo)


def _sigmoid(x):
    return 1.0 / (1.0 + jnp.exp(-x))


def _gelu_tanh(x):
    c = math.sqrt(2.0 / math.pi)
    return 0.5 * x * (1.0 + jnp.tanh(c * (x + 0.044715 * (x * x * x))))


def _mod_index(tile, n_prompt_tiles, tiles_per_latent_seq):
    return jnp.where(tile < n_prompt_tiles, 0, 1 + (tile - n_prompt_tiles) // tiles_per_latent_seq)


def _ada_kernel(c_ref, w_ref, b_ref, o_ref):
    c = c_ref[...]
    o_ref[0] = _dot3(c * _sigmoid(c), w_ref[0]) + b_ref[0]


def _ada_params(cvec, ada_w, ada_b):
    depth, d, n = ada_w.shape
    rows = cvec.shape[0]
    tn = min(512, n)
    return pl.pallas_call(
        _ada_kernel,
        grid=(depth, n // tn),
        in_specs=[
            pl.BlockSpec((rows, d), lambda l, j: (0, 0)),
            pl.BlockSpec((1, d, tn), lambda l, j: (l, 0, j)),
            pl.BlockSpec((1, 1, tn), lambda l, j: (l, 0, j)),
        ],
        out_specs=pl.BlockSpec((1, rows, tn), lambda l, j: (l, 0, j)),
        out_shape=jax.ShapeDtypeStruct((depth, rows, n), F32),
        compiler_params=_cparams("arbitrary", "arbitrary"),
        name="ada_params",
    )(cvec, ada_w, ada_b.reshape(depth, 1, n))


def _norm_mod(x, g, sc, sh):
    ms = jnp.mean(x * x, axis=-1, keepdims=True)
    return (x * lax.rsqrt(ms + RMS_EPS) * g) * (1.0 + sc) + sh


def _norm_mod_kernel(x_ref, g_ref, sc_ref, sh_ref, o_ref):
    o_ref[...] = _norm_mod(x_ref[...], g_ref[...], sc_ref[0], sh_ref[0]).astype(o_ref.dtype)


def _norm_mod_call(x, g, sc, sh, geom, out_dtype):
    m, d = x.shape
    tm = ROW_TILE
    midx = functools.partial(_mod_index, n_prompt_tiles=geom["mp"] // tm,
                             tiles_per_latent_seq=geom["ls"] // tm)
    return pl.pallas_call(
        _norm_mod_kernel,
        grid=(m // tm,),
        in_specs=[
            pl.BlockSpec((tm, d), lambda i: (i, 0)),
            pl.BlockSpec((1, d), lambda i: (0, 0)),
            pl.BlockSpec((1, 1, d), lambda i: (midx(i), 0, 0)),
            pl.BlockSpec((1, 1, d), lambda i: (midx(i), 0, 0)),
        ],
        out_specs=pl.BlockSpec((tm, d), lambda i: (i, 0)),
        out_shape=jax.ShapeDtypeStruct((m, d), out_dtype),
        compiler_params=_cparams("arbitrary"),
        name="norm_mod",
    )(x, g.reshape(1, d), sc, sh)


def _final_norm_kernel(x_ref, g_ref, o_ref):
    x = x_ref[...]
    ms = jnp.mean(x * x, axis=-1, keepdims=True)
    o_ref[...] = x * lax.rsqrt(ms + RMS_EPS) * g_ref[...]


def _final_norm(x, g, row0, n_rows):
    d = x.shape[1]
    tm = ROW_TILE
    return pl.pallas_call(
        _final_norm_kernel,
        grid=(n_rows // tm,),
        in_specs=[
            pl.BlockSpec((tm, d), lambda i: (row0 // tm + i, 0)),
            pl.BlockSpec((1, d), lambda i: (0, 0)),
        ],
        out_specs=pl.BlockSpec((tm, d), lambda i: (i, 0)),
        out_shape=jax.ShapeDtypeStruct((n_rows, d), F32),
        compiler_params=_cparams("arbitrary"),
        name="final_norm",
    )(x, g.astype(F32).reshape(1, d))


def _mm_bias_kernel(x_ref, w_ref, b_ref, o_ref):
    o_ref[...] = _dot(x_ref[...], w_ref[...]) + b_ref[...]


def _mm_bias(x, w, b):
    m, k = x.shape
    n = w.shape[1]
    tm = min(512, m)
    tn = n // 2 if (n // 2) % LANES == 0 else min(512, n)
    return pl.pallas_call(
        _mm_bias_kernel,
        grid=(n // tn, m // tm),
        in_specs=[
            pl.BlockSpec((tm, k), lambda j, i: (i, 0)),
            pl.BlockSpec((k, tn), lambda j, i: (0, j)),
            pl.BlockSpec((1, tn), lambda j, i: (0, j)),
        ],
        out_specs=pl.BlockSpec((tm, tn), lambda j, i: (i, j)),
        out_shape=jax.ShapeDtypeStruct((m, n), F32),
        compiler_params=_cparams("arbitrary", "arbitrary"),
        name="mm_bias",
    )(x, w, b.reshape(1, n))


def _mm_resid_kernel(a_ref, w_ref, x_ref, g_ref, o_ref):
    o_ref[...] = x_ref[...] + g_ref[0] * _dot(a_ref[...], w_ref[...])


def _mm_glu_resid_kernel(a_ref, wv_ref, wg_ref, x_ref, g_ref, o_ref):
    a = a_ref[...].astype(BF16)
    val = _dot(a, wv_ref[...])
    gate = _dot(a, wg_ref[...])
    o_ref[...] = x_ref[...] + g_ref[0] * (val * _sigmoid(gate))


def _mm_resid(a, w, x, gate, geom, glu):
    m, k = a.shape
    n = x.shape[1]
    tm = math.gcd(geom["mp"], geom["ls"], 2 * ROW_TILE)
    tn = min(1024, n)
    nj = n // tn
    midx = functools.partial(_mod_index, n_prompt_tiles=geom["mp"] // tm,
                             tiles_per_latent_seq=geom["ls"] // tm)
    a_spec = pl.BlockSpec((tm, k), lambda j, i: (i, 0))
    w_spec = pl.BlockSpec((k, tn), lambda j, i: (0, j))
    x_spec = pl.BlockSpec((tm, tn), lambda j, i: (i, j))
    g_spec = pl.BlockSpec((1, 1, tn), lambda j, i: (midx(i), 0, j))
    if glu:
        body = _mm_glu_resid_kernel
        in_specs = [a_spec, w_spec, pl.BlockSpec((k, tn), lambda j, i: (0, j + nj)), x_spec, g_spec]
        args = (a, w, w, x, gate)
    else:
        body = _mm_resid_kernel
        in_specs = [a_spec, w_spec, x_spec, g_spec]
        args = (a, w, x, gate)
    return pl.pallas_call(
        body,
        grid=(nj, m // tm),
        in_specs=in_specs,
        out_specs=pl.BlockSpec((tm, tn), lambda j, i: (i, j)),
        out_shape=jax.ShapeDtypeStruct((m, n), F32),
        compiler_params=_cparams("arbitrary", "arbitrary"),
        name="mm_glu_resid" if glu else "mm_resid",
    )(*args)


def _s5_operands(lam_re, lam_im, log_step, b_re, b_im, c_re, c_im, d_skip):
    t = S5_CHUNK
    n_groups, n_state = lam_re.shape[1:]
    ch = GROUP_CH
    tc = t * ch
    lam = lax.complex(jnp.minimum(lam_re.astype(F32), -1e-4), lam_im.astype(F32))
    step = jnp.exp(log_step.astype(F32))[..., None]
    lam_bar = jnp.exp(lam * step)
    b_bar = ((lam_bar - 1.0) / lam)[..., None] * lax.complex(b_re.astype(F32), b_im.astype(F32))
    c_mat = lax.complex(c_re.astype(F32), c_im.astype(F32))
    pw = [jnp.ones_like(lam_bar)]
    for _ in range(t):
        pw.append(pw[-1] * lam_bar)
    pw = jnp.stack(pw, axis=1)

    w1_f = jnp.einsum("tgp,gph->gthp", pw[0, t - 1::-1][:t], b_bar[0]).reshape(n_groups, tc, n_state)
    w1_b = jnp.einsum("tgp,gph->gthp", pw[1, :t], b_bar[1]).reshape(n_groups, tc, n_state)
    zeros = jnp.zeros((n_groups, tc, n_state), F32)

    def pair_cols(m, gi):
        return jnp.concatenate([m, zeros] if gi == 0 else [zeros, m], axis=-1)

    def w1_for(gi):
        sel = slice(gi, None, 2)
        return jnp.concatenate([
            pair_cols(jnp.real(w1_f), gi)[sel], pair_cols(jnp.imag(w1_f), gi)[sel],
            pair_cols(jnp.real(w1_b), gi)[sel], pair_cols(jnp.imag(w1_b), gi)[sel]], axis=-1)

    w1 = jnp.stack([w1_for(0), w1_for(1)], axis=1)

    k_f = jnp.real(jnp.einsum("gop,kgp,gpi->gkio", c_mat[0], pw[0, :t], b_bar[0]))
    k_b = jnp.real(jnp.einsum("gop,kgp,gpi->gkio", c_mat[1], pw[1, :t], b_bar[1]))
    s_idx = jnp.arange(t)[:, None]
    t_idx = jnp.arange(t)[None, :]
    lag_f = t_idx - s_idx
    lag_b = s_idx - t_idx
    def toeplitz(k, lag):
        out = jnp.zeros((n_groups, t, t, ch, ch), F32)
        for kk in range(t):
            out = out + jnp.where((lag == kk)[None, :, :, None, None], k[:, kk][:, None, None], 0.0)
        return out

    toep_f = toeplitz(k_f, lag_f)
    toep_b = toeplitz(k_b, lag_b)
    eye_t = jnp.eye(t, dtype=F32)[None, :, :, None, None]
    skip = eye_t * (d_skip.astype(F32).reshape(n_groups, ch)[:, None, None, :, None]
                    * jnp.eye(ch, dtype=F32)[None, None, None])
    toep = (toep_f + toep_b + skip).transpose(0, 1, 3, 2, 4).reshape(n_groups, tc, tc)
    cf = jnp.einsum("gop,tgp->gpto", c_mat[0], pw[0, 1:t + 1]).reshape(n_groups, n_state, tc)
    cb = jnp.einsum("gop,tgp->gpto", c_mat[1], pw[1, t:0:-1]).reshape(n_groups, n_state, tc)
    zrow = jnp.zeros((n_groups, n_state, tc), F32)

    def pair_rows(m):
        even = jnp.concatenate([m, zrow], axis=1)
        odd = jnp.concatenate([zrow, m], axis=1)
        return jnp.where((jnp.arange(n_groups) % 2 == 0)[:, None, None], even, odd)

    w3 = jnp.concatenate([toep, pair_rows(jnp.real(cf)), pair_rows(-jnp.imag(cf)),
                          pair_rows(jnp.real(cb)), pair_rows(-jnp.imag(cb))], axis=1).astype(BF16)

    a_pow = pw[:, t]
    a_vec = jnp.stack([jnp.real(a_pow[0]), jnp.imag(a_pow[0]), jnp.real(a_pow[1]), jnp.imag(a_pow[1])],
                      axis=1).reshape(n_groups // 2, 2, 4, n_state)
    a_vec = a_vec.transpose(0, 2, 1, 3).reshape(n_groups // 2, 4, 2 * n_state)
    return w1, w3, a_vec


def _complex_step(a_re, a_im, h_re, h_im, s_re, s_im):
    return a_re * h_re - a_im * h_im + s_re, a_re * h_im + a_im * h_re + s_im


def _s5_scan(s_scr, h_scr, a, init, n_chunks, rows_per_chunk):
    pl_ = LANES
    if rows_per_chunk % SUBLANES == 0:
        rt = rows_per_chunk

        def body(c, carry):
            f_re, f_im, g_re, g_im = carry
            rf = pl.ds(pl.multiple_of(c * rt, SUBLANES), rt)
            rb = pl.ds(pl.multiple_of((n_chunks - 1 - c) * rt, SUBLANES), rt)
            h_scr[0, rf, :] = f_re
            h_scr[1, rf, :] = f_im
            h_scr[2, rb, :] = g_re
            h_scr[3, rb, :] = g_im
            f_re, f_im = _complex_step(a[0:1], a[1:2], f_re, f_im,
                                       s_scr[rf, 0:pl_], s_scr[rf, pl_:2 * pl_])
            g_re, g_im = _complex_step(a[2:3], a[3:4], g_re, g_im,
                                       s_scr[rb, 2 * pl_:3 * pl_], s_scr[rb, 3 * pl_:4 * pl_])
            return f_re, f_im, g_re, g_im

        return lax.fori_loop(0, n_chunks, body, tuple(init))

    assert rows_per_chunk * 2 == SUBLANES and n_chunks % 2 == 0
    half = rows_per_chunk
    n_tiles = n_chunks // 2
    low = lax.broadcasted_iota(I32, (SUBLANES, pl_), 0) < half

    def body(j, carry):
        f_re, f_im, g_re, g_im = carry
        rf = pl.ds(pl.multiple_of(j * SUBLANES, SUBLANES), SUBLANES)
        rb = pl.ds(pl.multiple_of((n_tiles - 1 - j) * SUBLANES, SUBLANES), SUBLANES)
        sf_re, sf_im = s_scr[rf, 0:pl_], s_scr[rf, pl_:2 * pl_]
        sb_re, sb_im = s_scr[rb, 2 * pl_:3 * pl_], s_scr[rb, 3 * pl_:4 * pl_]
        f1_re, f1_im = _complex_step(a[0:1], a[1:2], f_re, f_im, sf_re, sf_im)
        g1_re, g1_im = _complex_step(a[2:3], a[3:4], g_re, g_im, sb_re, sb_im)
        f1s_re, f1s_im = pltpu.roll(f1_re, half, 0), pltpu.roll(f1_im, half, 0)
        g1s_re, g1s_im = pltpu.roll(g1_re, half, 0), pltpu.roll(g1_im, half, 0)
        h_scr[0, rf, :] = jnp.where(low, f_re, f1s_re)
        h_scr[1, rf, :] = jnp.where(low, f_im, f1s_im)
        h_scr[2, rb, :] = jnp.where(low, g1s_re, g_re)
        h_scr[3, rb, :] = jnp.where(low, g1s_im, g_im)
        f2_re, f2_im = _complex_step(a[0:1], a[1:2], f1s_re, f1s_im, sf_re, sf_im)
        g2_re, g2_im = _complex_step(a[2:3], a[3:4], g1s_re, g1s_im, sb_re, sb_im)
        return (pltpu.roll(f2_re, half, 0), pltpu.roll(f2_im, half, 0),
                pltpu.roll(g2_re, half, 0), pltpu.roll(g2_im, half, 0))

    return lax.fori_loop(0, n_tiles, body, tuple(init))


def _s5_kernel(up_ref, us_ref, w1_ref, w3_ref, a_ref, h0_ref, gp_ref, gs_ref, fin_ref,
               sp_scr, ss_scr, hp_scr, hs_scr, *, geom):
    tc = up_ref.shape[-1]
    a = a_ref[0]
    w1 = [w1_ref[0, 0], w1_ref[0, 1]]
    w1_hi = [w.astype(BF16) for w in w1]

    sp_scr[...] = _dot3(up_ref[0], w1[0]) + _dot3(up_ref[1], w1[1])
    ss_scr[...] = _dot(us_ref[0].astype(BF16), w1_hi[0]) + _dot(us_ref[1].astype(BF16), w1_hi[1])

    zero = jnp.zeros((geom["bp"], LANES), F32)
    fin = _s5_scan(sp_scr, hp_scr, a, (zero, zero, zero, zero), geom["lp"] // S5_CHUNK, geom["bp"])
    for comp in range(4):
        fin_ref[0, comp] = fin[comp]
    _s5_scan(ss_scr, hs_scr, a, tuple(h0_ref[0, comp] for comp in range(4)),
             geom["ls"] // S5_CHUNK, geom["bs"])

    for u_ref, h_scr, o_ref in ((up_ref, hp_scr, gp_ref), (us_ref, hs_scr, gs_ref)):
        for gi in range(2):
            y = _dot(u_ref[gi].astype(BF16), w3_ref[gi, 0:tc, :])
            for comp in range(4):
                y += _dot(h_scr[comp].astype(BF16),
                          w3_ref[gi, tc + comp * LANES:tc + (comp + 1) * LANES, :])
            o_ref[gi] = _gelu_tanh(y).astype(o_ref.dtype)


def _s5_mixer(h, h0_re, h0_im, params, geom):
    m, d = h.shape
    t = S5_CHUNK
    ch = GROUP_CH
    n_groups = d // ch
    tc = t * ch
    bp, lp, bs, ls, mp = geom["bp"], geom["lp"], geom["bs"], geom["ls"], geom["mp"]
    rp, rs = mp // t, (m - mp) // t
    w1, w3, a_vec = _s5_operands(*params)
    n_state = a_vec.shape[-1] // 2

    def to_chunks(x, b, l):
        x = x.reshape(b, l // t, t, n_groups, ch).transpose(3, 1, 0, 2, 4)
        return x.reshape(n_groups, (l // t) * b, tc)

    def from_chunks(x, b, l):
        x = x.reshape(n_groups, l // t, b, t, ch).transpose(2, 1, 3, 0, 4)
        return x.reshape(b * l, d)

    up = to_chunks(h[:mp], bp, lp)
    us = to_chunks(h[mp:], bs, ls)
    h0 = jnp.stack([h0_re[:, 0], h0_im[:, 0], h0_re[:, 1], h0_im[:, 1]], axis=0).astype(F32)
    h0 = h0.reshape(4, bs, n_groups // 2, 2 * n_state).transpose(2, 0, 1, 3)
    reps = max(1, SUBLANES // bs)
    h0 = jnp.tile(h0, (1, 1, reps, 1))
    h0_rows = h0.shape[2]

    gp, gs, fin = pl.pallas_call(
        functools.partial(_s5_kernel, geom=geom),
        grid=(n_groups // 2,),
        in_specs=[
            pl.BlockSpec((2, rp, tc), lambda i: (i, 0, 0)),
            pl.BlockSpec((2, rs, tc), lambda i: (i, 0, 0)),
            pl.BlockSpec((1, 2, tc, 4 * LANES), lambda i: (i, 0, 0, 0)),
            pl.BlockSpec((2, tc + 4 * LANES, tc), lambda i: (i, 0, 0)),
            pl.BlockSpec((1, 4, LANES), lambda i: (i, 0, 0)),
            pl.BlockSpec((1, 4, h0_rows, LANES), lambda i: (i, 0, 0, 0)),
        ],
        out_specs=[
            pl.BlockSpec((2, rp, tc), lambda i: (i, 0, 0)),
            pl.BlockSpec((2, rs, tc), lambda i: (i, 0, 0)),
            pl.BlockSpec((1, 4, bp, LANES), lambda i: (i, 0, 0, 0)),
        ],
        out_shape=[
            jax.ShapeDtypeStruct((n_groups, rp, tc), F32),
            jax.ShapeDtypeStruct((n_groups, rs, tc), F32),
            jax.ShapeDtypeStruct((n_groups // 2, 4, bp, LANES), F32),
        ],
        scratch_shapes=[
            pltpu.VMEM((rp, 4 * LANES), F32),
            pltpu.VMEM((rs, 4 * LANES), F32),
            pltpu.VMEM((4, rp, LANES), F32),
            pltpu.VMEM((4, rs, LANES), F32),
        ],
        compiler_params=_cparams("arbitrary"),
        name="s5_mixer",
    )(up, us, w1, w3, a_vec, h0)

    g = jnp.concatenate([from_chunks(gp, bp, lp), from_chunks(gs, bs, ls)], axis=0)
    fin = fin.reshape(n_groups // 2, 2, 2, bp, 2, n_state).transpose(3, 1, 2, 0, 4, 5)
    fin = fin.reshape(bp, 2, 2, n_groups, n_state)
    return g, fin[:, :, 0], fin[:, :, 1]


def _softmax_with_sink(scores, sink):
    mx = sink
    for s in scores:
        mx = jnp.maximum(mx, jnp.max(s, axis=-1, keepdims=True))
    ps = [jnp.exp(s - mx) for s in scores]
    den = jnp.exp(sink - mx)
    for p in ps:
        den = den + jnp.sum(p, axis=-1, keepdims=True)
    return ps, 1.0 / den


def _attn_prompt_kernel(sink_ref, q_ref, kv_ref, o_ref, *, n_kv, q_per_kv):
    hd = HEAD_DIM
    scale = hd ** -0.5
    for kv in range(n_kv):
        k = kv_ref[:, kv * hd:(kv + 1) * hd].astype(BF16)
        v = kv_ref[:, (n_kv + kv) * hd:(n_kv + kv + 1) * hd].astype(BF16)
        for gq in range(q_per_kv):
            head = kv * q_per_kv + gq
            q = (q_ref[:, head * hd:(head + 1) * hd] * scale).astype(BF16)
            s = lax.dot_general(q, k, _NT, preferred_element_type=F32)
            (p,), inv = _softmax_with_sink([s], sink_ref[head])
            o_ref[:, head * hd:(head + 1) * hd] = (_dot(p.astype(BF16), v) * inv).astype(o_ref.dtype)


def _attn_prompt(qkv, sink, geom, n_kv):
    d = geom["d"]
    bp, lp = geom["bp"], geom["lp"]
    kvw = 2 * n_kv * HEAD_DIM
    n_heads = d // HEAD_DIM
    return pl.pallas_call(
        functools.partial(_attn_prompt_kernel, n_kv=n_kv, q_per_kv=n_heads // n_kv),
        grid=(bp,),
        in_specs=[
            pl.BlockSpec(memory_space=pltpu.SMEM),
            pl.BlockSpec((lp, d), lambda b: (b, 0)),
            pl.BlockSpec((lp, kvw), lambda b: (b, d // kvw)),
        ],
        out_specs=pl.BlockSpec((lp, d), lambda b: (b, 0)),
        out_shape=jax.ShapeDtypeStruct((bp * lp, d), BF16),
        compiler_params=_cparams("arbitrary"),
        name="attn_prompt",
    )(sink, qkv, qkv)


def _rope_tables(length):
    pairs = HEAD_DIM // 4
    pos = jnp.arange(length)
    row = (pos // GRID_W).astype(F32)
    col = (pos % GRID_W).astype(F32)
    inv_freq = ROPE_BASE ** (-jnp.arange(pairs, dtype=F32) / pairs)
    lane = jnp.arange(LANES) % HEAD_DIM
    axis_pos = jnp.where((lane // (2 * pairs))[None, :] == 0, row[:, None], col[:, None])
    ang = axis_pos * inv_freq[lane % pairs][None, :]
    sign = jnp.where((lane % (2 * pairs)) < pairs, -1.0, 1.0)[None, :]
    return jnp.cos(ang), jnp.sin(ang) * sign


def _rope(x, cos, sin):
    w = x.shape[1]
    pairs = HEAD_DIM // 4
    reps = w // LANES
    cos_w = jnp.concatenate([cos] * reps, axis=1) if reps > 1 else cos
    sin_w = jnp.concatenate([sin] * reps, axis=1) if reps > 1 else sin
    lane = lax.broadcasted_iota(I32, x.shape, 1)
    partner = jnp.where(lane % (2 * pairs) < pairs, pltpu.roll(x, w - pairs, 1), pltpu.roll(x, pairs, 1))
    return x * cos_w + partner * sin_w


def _attn_latent_kernel(sink_ref, q_ref, kvp_ref, kvo_ref, kvn_ref, ck_ref, cv_ref,
                        cos_p, sin_p, cos_o, sin_o, cos_n, sin_n, o_ref, *, n_kv, q_per_kv, n_blocks):
    hd = HEAD_DIM
    blk = ATTN_BLOCK
    scale = hd ** -0.5
    i = pl.program_id(1)
    kw = n_kv * hd
    k_win = jnp.concatenate([
        _rope(kvp_ref[:, 0:kw], cos_p[...], sin_p[...]),
        _rope(kvo_ref[:, 0:kw], cos_o[...], sin_o[...]),
        _rope(kvn_ref[:, 0:kw], cos_n[...], sin_n[...])], axis=0).astype(BF16)
    v_win = jnp.concatenate([kvp_ref[:, kw:2 * kw], kvo_ref[:, kw:2 * kw], kvn_ref[:, kw:2 * kw]],
                            axis=0).astype(BF16)
    rows = q_per_kv * blk
    r_in_blk = lax.broadcasted_iota(I32, (rows, 3 * blk), 0) % blk
    c_idx = lax.broadcasted_iota(I32, (rows, 3 * blk), 1)
    k_pos = (i - 1) * blk + c_idx
    mask = (jnp.abs(c_idx - blk - r_in_blk) <= WINDOW) & (k_pos >= 0) & (k_pos < n_blocks * blk)
    for kv in range(n_kv):
        qg = _rope(q_ref[:, kv * q_per_kv * hd:(kv + 1) * q_per_kv * hd], cos_o[...], sin_o[...]) * scale
        q = jnp.concatenate([qg[:, gq * hd:(gq + 1) * hd] for gq in range(q_per_kv)], axis=0).astype(BF16)
        sink = jnp.concatenate(
            [jnp.full((blk, 1), sink_ref[kv * q_per_kv + gq], F32) for gq in range(q_per_kv)], axis=0)
        s_ctx = lax.dot_general(q, ck_ref[0, :, kv * hd:(kv + 1) * hd].astype(BF16), _NT,
                                preferred_element_type=F32)
        s_win = lax.dot_general(q, k_win[:, kv * hd:(kv + 1) * hd], _NT, preferred_element_type=F32)
        s_win = jnp.where(mask, s_win, NEG_INF)
        (p_ctx, p_win), inv = _softmax_with_sink([s_ctx, s_win], sink)
        o = (_dot(p_ctx.astype(BF16), cv_ref[0, :, kv * hd:(kv + 1) * hd].astype(BF16))
             + _dot(p_win.astype(BF16), v_win[:, kv * hd:(kv + 1) * hd])) * inv
        for gq in range(q_per_kv):
            head = kv * q_per_kv + gq
            o_ref[:, head * hd:(head + 1) * hd] = o[gq * blk:(gq + 1) * blk].astype(o_ref.dtype)


def _attn_latent(qkv, ctx_k, ctx_v, sink, geom, n_kv):
    d = geom["d"]
    bs, ls, mp = geom["bs"], geom["ls"], geom["mp"]
    blk = ATTN_BLOCK
    nb = ls // blk
    kvw = 2 * n_kv * HEAD_DIM
    n_heads = d // HEAD_DIM
    past = ctx_k.shape[1]
    ck = ctx_k.reshape(bs, past, n_kv * HEAD_DIM).astype(F32)
    cv = ctx_v.reshape(bs, past, n_kv * HEAD_DIM).astype(F32)
    cos, sin = _rope_tables(ls)
    base = mp // blk

    def prev(i):
        return jnp.maximum(i - 1, 0)

    def nxt(i):
        return jnp.minimum(i + 1, nb - 1)

    def kv_spec(f):
        return pl.BlockSpec((blk, kvw), lambda b, i: (base + b * nb + f(i), d // kvw))

    def tab_spec(f):
        return pl.BlockSpec((blk, LANES), lambda b, i: (f(i), 0))

    same = lambda i: i
    return pl.pallas_call(
        functools.partial(_attn_latent_kernel, n_kv=n_kv, q_per_kv=n_heads // n_kv, n_blocks=nb),
        grid=(bs, nb),
        in_specs=[
            pl.BlockSpec(memory_space=pltpu.SMEM),
            pl.BlockSpec((blk, d), lambda b, i: (base + b * nb + i, 0)),
            kv_spec(prev), kv_spec(same), kv_spec(nxt),
            pl.BlockSpec((1, past, n_kv * HEAD_DIM), lambda b, i: (b, 0, 0)),
            pl.BlockSpec((1, past, n_kv * HEAD_DIM), lambda b, i: (b, 0, 0)),
            tab_spec(prev), tab_spec(prev), tab_spec(same), tab_spec(same), tab_spec(nxt), tab_spec(nxt),
        ],
        out_specs=pl.BlockSpec((blk, d), lambda b, i: (b * nb + i, 0)),
        out_shape=jax.ShapeDtypeStruct((bs * ls, d), BF16),
        compiler_params=_cparams("arbitrary", "arbitrary"),
        name="attn_latent",
    )(sink, qkv, qkv, qkv, qkv, ck, cv, cos, sin, cos, sin, cos, sin)


def _pack_bf16_pairs(x):
    n = x.shape[1] // 2

    def rounded_bits(v):
        return lax.bitcast_convert_type(v, U32) + jnp.uint32(0x8000)

    return (rounded_bits(x[:, :n]) & jnp.uint32(0xFFFF0000)) | (rounded_bits(x[:, n:]) >> 16)


def _store_row_tiles(ref, words):
    for s in range(ref.shape[1]):
        ref[:, s, :] = words[:, s * LANES:(s + 1) * LANES]


def _unpack_hi(words):
    return lax.bitcast_convert_type(words & jnp.uint32(0xFFFF0000), F32)


def _unpack_lo(words):
    return lax.bitcast_convert_type(words << 16, F32)


def _load_row_tiles_bf16(ref, ns):
    rows = ref.shape[0] // ns
    tiles = [ref[pl.ds(s, rows, stride=ns), :] for s in range(ns)]
    return jnp.concatenate([_unpack_hi(t).astype(BF16) for t in tiles]
                           + [_unpack_lo(t).astype(BF16) for t in tiles], axis=1)


def _router_kernel(x_ref, g_ref, sc_ref, sh_ref, wt_hi_ref, wt_lo_ref, bias_ref,
                   h3_ref, lposd_ref, lposp_ref, w_ref, win_ref, cnt_ref, base_ref, base_scr, base_col_scr,
                   *, n_experts):
    step = pl.program_id(0)

    @pl.when(step == 0)
    def _():
        base_scr[...] = jnp.zeros_like(base_scr)
        base_col_scr[...] = jnp.zeros_like(base_col_scr)

    h = _norm_mod(x_ref[...], g_ref[...], sc_ref[0], sh_ref[0])
    _store_row_tiles(h3_ref, _pack_bf16_pairs(h))
    tm = h.shape[0]
    h_hi, h_lo = _split(h)
    wt_hi = wt_hi_ref[...]
    logits = (lax.dot_general(wt_hi, h_hi, _NT, preferred_element_type=F32)
              + lax.dot_general(wt_lo_ref[...], h_hi, _NT, preferred_element_type=F32)
              + lax.dot_general(wt_hi, h_lo, _NT, preferred_element_type=F32))
    scores = _sigmoid(logits)
    sel = scores + bias_ref[:, 0:1]
    per_group = n_experts // N_EXPERT_GROUPS
    e_iota = lax.broadcasted_iota(I32, (n_experts, tm), 0)
    big = jnp.int32(1 << 30)

    grp = []
    for gidx in range(N_EXPERT_GROUPS):
        v = sel[gidx * per_group:(gidx + 1) * per_group]
        r = lax.broadcasted_iota(I32, v.shape, 0) + gidx * per_group
        m1 = jnp.max(v, axis=0, keepdims=True)
        i1 = jnp.min(jnp.where(v == m1, r, big), axis=0, keepdims=True)
        m2 = jnp.max(jnp.where(r == i1, -jnp.inf, v), axis=0, keepdims=True)
        grp.append(jnp.broadcast_to(m1 + m2, v.shape))
    cur = jnp.concatenate(grp, axis=0)
    g_iota = e_iota // per_group
    cand = jnp.full(sel.shape, -jnp.inf, F32)
    for _ in range(TOPK_GROUPS):
        mx = jnp.max(cur, axis=0, keepdims=True)
        gi = jnp.min(jnp.where(cur == mx, g_iota, big), axis=0, keepdims=True)
        hit = g_iota == gi
        cand = jnp.where(hit, sel, cand)
        cur = jnp.where(hit, -jnp.inf, cur)

    idxs, wts = [], []
    onehot = jnp.zeros((n_experts, tm), F32)
    for _ in range(TOP_K):
        mx = jnp.max(cand, axis=0, keepdims=True)
        ei = jnp.min(jnp.where(cand == mx, e_iota, big), axis=0, keepdims=True)
        hit = e_iota == ei
        idxs.append(ei)
        wts.append(jnp.sum(jnp.where(hit, scores, 0.0), axis=0, keepdims=True))
        onehot = jnp.where(hit, 1.0, onehot)
        cand = jnp.where(hit, -jnp.inf, cand)
    wsum = wts[0]
    for wk in wts[1:]:
        wsum = wsum + wk
    w_ref[0] = jnp.concatenate(wts, axis=0) / wsum * ROUTED_SCALE

    onehot_b = onehot.astype(BF16)
    t_row = lax.broadcasted_iota(I32, (tm, tm), 0)
    t_col = lax.broadcasted_iota(I32, (tm, tm), 1)
    earlier_tok = _dot(onehot_b, jnp.where(t_row < t_col, 1.0, 0.0).astype(BF16))
    cnt = jnp.sum(onehot, axis=1, keepdims=True)
    cnt_pad = jnp.floor((cnt + (DMA_WINDOW - 1)) * (1.0 / DMA_WINDOW)) * DMA_WINDOW
    e_row = lax.broadcasted_iota(I32, (n_experts, n_experts), 0)
    e_col = lax.broadcasted_iota(I32, (n_experts, n_experts), 1)
    lower = jnp.where(e_col < e_row, 1.0, 0.0).astype(BF16)
    start_d = _dot(lower, jnp.broadcast_to(cnt, (n_experts, LANES)).astype(BF16))[:, 0:1]
    start_p = _dot(lower, jnp.broadcast_to(cnt_pad, (n_experts, LANES)).astype(BF16))[:, 0:1]
    pos_d = start_d + earlier_tok
    pos_p = start_p + earlier_tok
    lposd_ref[0] = jnp.concatenate(
        [jnp.sum(jnp.where(e_iota == ei, pos_d, 0.0), axis=0, keepdims=True) for ei in idxs], axis=0).astype(I32)
    lposp_ref[0] = jnp.concatenate(
        [jnp.sum(jnp.where(e_iota == ei, pos_p, 0.0), axis=0, keepdims=True) for ei in idxs], axis=0).astype(I32)

    n_win_max = win_ref.shape[2]
    nw = cnt_pad * (1.0 / DMA_WINDOW)
    w_start = _dot(lower, jnp.broadcast_to(nw, (n_experts, LANES)).astype(BF16))[:, 0:1]
    q = lax.broadcasted_iota(I32, (n_experts, n_win_max), 1).astype(F32)
    w_expert = jnp.sum(jnp.where(w_start + nw <= q, 1.0, 0.0), axis=0, keepdims=True)
    mine = lax.broadcasted_iota(I32, (n_experts, n_win_max), 0).astype(F32) == w_expert

    def pick(col):
        return jnp.sum(jnp.where(mine, col, 0.0), axis=0, keepdims=True)

    q_row = lax.broadcasted_iota(I32, (1, n_win_max), 1).astype(F32)
    offset = (q_row - pick(w_start)) * DMA_WINDOW
    total = jnp.broadcast_to(jnp.sum(nw, axis=0, keepdims=True), (1, n_win_max))
    zero_row = jnp.zeros((1, n_win_max), F32)
    win_ref[0] = jnp.concatenate(
        [w_expert, pick(base_col_scr[:, 0:1]) + offset, pick(start_d) + offset, pick(start_p) + offset, total,
         zero_row, zero_row, zero_row], axis=0).astype(I32)
    base_col_scr[...] = base_col_scr[...] + cnt

    cnt_row = lax.dot_general(jnp.ones((SUBLANES, tm), BF16), onehot_b, _NT, preferred_element_type=F32)
    cnt_ref[0] = cnt_row
    base_ref[0] = base_scr[...]
    base_scr[...] = base_scr[...] + cnt_row


def _route(x, g, sc, sh, w_router, bias, geom):
    m, d = x.shape
    n_experts = w_router.shape[1]
    tm = MOE_TILE
    n_tiles = m // tm
    ns = d // (2 * LANES)
    wt = w_router.astype(F32).T
    wt_hi = wt.astype(BF16)
    wt_lo = (wt - wt_hi.astype(F32)).astype(BF16)
    bias_b = jnp.broadcast_to(bias.astype(F32)[:, None], (n_experts, LANES))
    midx = functools.partial(_mod_index, n_prompt_tiles=geom["mp"] // tm,
                             tiles_per_latent_seq=geom["ls"] // tm)
    n_win_max = TOP_K * tm // DMA_WINDOW + n_experts
    tok_spec = pl.BlockSpec((1, TOP_K, tm), lambda i: (i, 0, 0))
    win_spec = pl.BlockSpec((1, SUBLANES, n_win_max), lambda i: (i, 0, 0))
    exp_spec = pl.BlockSpec((1, SUBLANES, n_experts), lambda i: (i, 0, 0))
    return pl.pallas_call(
        functools.partial(_router_kernel, n_experts=n_experts),
        grid=(n_tiles,),
        in_specs=[
            pl.BlockSpec((tm, d), lambda i: (i, 0)),
            pl.BlockSpec((1, d), lambda i: (0, 0)),
            pl.BlockSpec((1, 1, d), lambda i: (midx(i), 0, 0)),
            pl.BlockSpec((1, 1, d), lambda i: (midx(i), 0, 0)),
            pl.BlockSpec((n_experts, d), lambda i: (0, 0)),
            pl.BlockSpec((n_experts, d), lambda i: (0, 0)),
            pl.BlockSpec((n_experts, LANES), lambda i: (0, 0)),
        ],
        out_specs=[
            pl.BlockSpec((tm, ns, LANES), lambda i: (i, 0, 0)),
            tok_spec, tok_spec, tok_spec, win_spec, exp_spec, exp_spec,
        ],
        out_shape=[
            jax.ShapeDtypeStruct((m, ns, LANES), U32),
            jax.ShapeDtypeStruct((n_tiles, TOP_K, tm), I32),
            jax.ShapeDtypeStruct((n_tiles, TOP_K, tm), I32),
            jax.ShapeDtypeStruct((n_tiles, TOP_K, tm), F32),
            jax.ShapeDtypeStruct((n_tiles, SUBLANES, n_win_max), I32),
            jax.ShapeDtypeStruct((n_tiles, SUBLANES, n_experts), F32),
            jax.ShapeDtypeStruct((n_tiles, SUBLANES, n_experts), F32),
        ],
        scratch_shapes=[pltpu.VMEM((SUBLANES, n_experts), F32), pltpu.VMEM((n_experts, LANES), F32)],
        compiler_params=_cparams("arbitrary"),
        name="moe_router",
    )(x, g.reshape(1, d), sc, sh, wt_hi, wt_lo, bias_b)


_FILL_SIZES = tuple(1 << b for b in range(EXPERT_TILE.bit_length() - 1, -1, -1))


def _drain_windows(n_win, n_win_max, copy_of_slots):
    max_chunk = TOP_K * MOE_TILE // DMA_WINDOW
    sizes = [max_chunk] * (n_win_max // max_chunk)
    size = max_chunk // 2
    while size >= 1:
        sizes.append(size)
        size //= 2
    left = n_win
    for s in sizes:
        take = left >= s

        @pl.when(take)
        def _():
            copy_of_slots(s * DMA_WINDOW).wait()

        left = left - jnp.where(take, s, 0)


def _dispatch_kernel(fill_start_ref, fill_n_ref, pad_start_ref, lpos_hbm, win_hbm, h3_ref, xs_hbm,
                     lpos_smem, win_smem, buf, zero_scr, idx_sem, row_sem, *, n_experts):
    step = pl.program_id(0)
    tm = h3_ref.shape[0]
    n_assign = TOP_K * tm
    win = DMA_WINDOW

    @pl.when(step == 0)
    def _():
        zero_scr[...] = jnp.zeros_like(zero_scr)
        buf[...] = jnp.zeros_like(buf)

        def fill_copies(e):
            start = fill_start_ref[e]
            n = fill_n_ref[e]
            off = jnp.int32(0)
            out = []
            for size in _FILL_SIZES:
                take = (n & size) != 0
                out.append((take, pltpu.make_async_copy(zero_scr.at[pl.ds(0, size)],
                                                        xs_hbm.at[pl.ds(start + off, size)], row_sem)))
                off = off + jnp.where(take, size, 0)
            return out

        def issue(e, carry):
            for take, cp in fill_copies(e):
                @pl.when(take)
                def _():
                    cp.start()
            return carry

        def drain(e, carry):
            for take, cp in fill_copies(e):
                @pl.when(take)
                def _():
                    cp.wait()
            return carry

        lax.fori_loop(0, n_experts, issue, 0)
        lax.fori_loop(0, n_experts, drain, 0)

    @pl.when(step > 0)
    def _():
        tile = step - 1
        idx_copies = [pltpu.make_async_copy(lpos_hbm.at[tile], lpos_smem, idx_sem.at[0]),
                      pltpu.make_async_copy(win_hbm.at[tile], win_smem, idx_sem.at[1])]
        for cp in idx_copies:
            cp.start()
        for cp in idx_copies:
            cp.wait()

        def per_token(t, carry):
            row = h3_ref[t]
            for k in range(TOP_K):
                buf[lpos_smem[k * tm + t]] = row
            return carry

        lax.fori_loop(0, tm, per_token, 0, unroll=2)

        def window_copy(src, dst):
            return pltpu.make_async_copy(buf.at[pl.ds(src, win)], xs_hbm.at[pl.ds(dst, win)], row_sem)

        n_win_max = win_smem.shape[0] // SUBLANES
        n_win = win_smem[4 * n_win_max]

        def issue(q, carry):
            dst = pad_start_ref[win_smem[q]] + win_smem[n_win_max + q]
            window_copy(win_smem[2 * n_win_max + q], dst).start()
            return carry

        lax.fori_loop(0, n_win, issue, 0)
        _drain_windows(n_win, n_win_max, lambda slots: pltpu.make_async_copy(
            buf.at[pl.ds(0, slots)], xs_hbm.at[pl.ds(0, slots)], row_sem))


def _dispatch(h3, lposd, wins, fill_start, fill_n, pad_start, n_slots):
    m, ns, _ = h3.shape
    tm = MOE_TILE
    n_tiles = m // tm
    n_experts = fill_start.shape[0]
    win_words = wins.shape[1] * wins.shape[2]

    return pl.pallas_call(
        functools.partial(_dispatch_kernel, n_experts=n_experts),
        grid_spec=pltpu.PrefetchScalarGridSpec(
            num_scalar_prefetch=3,
            grid=(n_tiles + 1,),
            in_specs=[
                pl.BlockSpec(memory_space=pl.ANY),
                pl.BlockSpec(memory_space=pl.ANY),
                pl.BlockSpec((tm, ns, LANES), lambda i, fs, fn, ps: (jnp.maximum(i - 1, 0), 0, 0)),
            ],
            out_specs=pl.BlockSpec(memory_space=pl.ANY),
            scratch_shapes=[
                pltpu.SMEM((TOP_K * tm,), I32),
                pltpu.SMEM((win_words,), I32),
                pltpu.VMEM((TOP_K * tm + DMA_WINDOW, ns, LANES), U32),
                pltpu.VMEM((EXPERT_TILE, ns, LANES), U32),
                pltpu.SemaphoreType.DMA((2,)),
                pltpu.SemaphoreType.DMA,
            ],
        ),
        out_shape=jax.ShapeDtypeStruct((n_slots, ns, LANES), U32),
        compiler_params=_cparams("arbitrary"),
        name="moe_dispatch",
    )(fill_start, fill_n, pad_start, lposd.reshape(n_tiles, TOP_K * tm), wins.reshape(n_tiles, win_words), h3)


def _expert_kernel(be_ref, first_ref, nxt_ref, par_ref, na_ref, xs_ref, wg_hbm, wu_hbm, wd_hbm, y_ref,
                   wg_buf, wu_buf, wd_buf, wg_scr, wu_scr, wd_scr, sem, *, layer):
    i = pl.program_id(0)

    def fetch(expert, slot):
        return [pltpu.make_async_copy(src.at[layer, expert], dst.at[slot], sem.at[slot, j])
                for j, (src, dst) in enumerate(((wg_hbm, wg_buf), (wu_hbm, wu_buf), (wd_hbm, wd_buf)))]

    @pl.when(i < na_ref[0])
    def _():
        slot = par_ref[i]

        @pl.when(i == 0)
        def _():
            for cp in fetch(be_ref[0], slot):
                cp.start()

        @pl.when(first_ref[i] == 1)
        def _():
            @pl.when(nxt_ref[i] >= 0)
            def _():
                for cp in fetch(nxt_ref[i], 1 - slot):
                    cp.start()

            for cp in fetch(be_ref[i], slot):
                cp.wait()
            wg_scr[...] = wg_buf[slot].astype(BF16)
            wu_scr[...] = wu_buf[slot].astype(BF16)
            wd_scr[...] = wd_buf[slot].astype(BF16)

        x = _load_row_tiles_bf16(xs_ref, y_ref.shape[1])
        gate = _dot(x, wg_scr[...])
        up = _dot(x, wu_scr[...])
        act = (gate * _sigmoid(gate) * up).astype(BF16)
        _store_row_tiles(y_ref, _pack_bf16_pairs(_dot(act, wd_scr[...])))


def _experts(xs3, sched, w_gate, w_up, w_down, layer):
    n_slots, ns, _ = xs3.shape
    tm = EXPERT_TILE
    d, de = w_gate.shape[-2:]
    n_blocks = n_slots // tm

    def blk(i, be, first, nxt, par, na):
        return (jnp.minimum(i, na[0] - 1), 0, 0)

    return pl.pallas_call(
        functools.partial(_expert_kernel, layer=layer),
        grid_spec=pltpu.PrefetchScalarGridSpec(
            num_scalar_prefetch=5,
            grid=(n_blocks,),
            in_specs=[
                pl.BlockSpec((tm * ns, LANES), lambda *a: blk(*a)[:2]),
                pl.BlockSpec(memory_space=pl.ANY),
                pl.BlockSpec(memory_space=pl.ANY),
                pl.BlockSpec(memory_space=pl.ANY),
            ],
            out_specs=pl.BlockSpec((tm, ns, LANES), blk),
            scratch_shapes=[
                pltpu.VMEM((2, d, de), F32),
                pltpu.VMEM((2, d, de), F32),
                pltpu.VMEM((2, de, d), F32),
                pltpu.VMEM((d, de), BF16),
                pltpu.VMEM((d, de), BF16),
                pltpu.VMEM((de, d), BF16),
                pltpu.SemaphoreType.DMA((2, 3)),
            ],
        ),
        out_shape=jax.ShapeDtypeStruct((n_slots, ns, LANES), U32),
        compiler_params=_cparams("arbitrary"),
        name="moe_experts",
    )(*sched, xs3.reshape(n_slots * ns, LANES), w_gate, w_up, w_down)


def _combine_kernel(pad_start_ref, lpos_hbm, w_hbm, win_hbm, y_hbm, x_ref, h3_ref, g_ref, sg_ref, su_ref, sd_ref,
                    o_ref, lpos_smem, w_smem, win_smem, ybuf, rout, idx_sem, row_sem):
    step = pl.program_id(0)
    tm = x_ref.shape[0]
    ns = ybuf.shape[1]
    win = DMA_WINDOW
    idx_copies = [pltpu.make_async_copy(lpos_hbm.at[step], lpos_smem, idx_sem.at[0]),
                  pltpu.make_async_copy(w_hbm.at[step], w_smem, idx_sem.at[1]),
                  pltpu.make_async_copy(win_hbm.at[step], win_smem, idx_sem.at[2])]
    for cp in idx_copies:
        cp.start()

    @pl.when(step == 0)
    def _():
        ybuf[...] = jnp.zeros_like(ybuf)

    for cp in idx_copies:
        cp.wait()

    def window_copy(src, dst):
        return pltpu.make_async_copy(y_hbm.at[pl.ds(src, win)], ybuf.at[pl.ds(dst, win)], row_sem)

    n_win_max = win_smem.shape[0] // SUBLANES
    n_win = win_smem[4 * n_win_max]

    def issue(q, carry):
        src = pad_start_ref[win_smem[q]] + win_smem[n_win_max + q]
        window_copy(src, win_smem[3 * n_win_max + q]).start()
        return carry

    lax.fori_loop(0, n_win, issue, 0)

    hb = _load_row_tiles_bf16(h3_ref, ns)
    gate = _dot(hb, sg_ref[...])
    up = _dot(hb, su_ref[...])
    shared = _dot((gate * _sigmoid(gate) * up).astype(BF16), sd_ref[...])

    _drain_windows(n_win, n_win_max, lambda slots: pltpu.make_async_copy(
        y_hbm.at[pl.ds(0, slots)], ybuf.at[pl.ds(0, slots)], row_sem))

    def per_token(t, carry):
        acc_hi = jnp.zeros((ns, LANES), F32)
        acc_lo = jnp.zeros((ns, LANES), F32)
        for k in range(TOP_K):
            words = ybuf[lpos_smem[k * tm + t]]
            wk = w_smem[k * tm + t]
            acc_hi = acc_hi + wk * _unpack_hi(words)
            acc_lo = acc_lo + wk * _unpack_lo(words)
        rout[pl.ds(pl.multiple_of(t * 2 * ns, ns), ns), :] = acc_hi
        rout[pl.ds(pl.multiple_of(t * 2 * ns + ns, ns), ns), :] = acc_lo
        return carry

    lax.fori_loop(0, tm, per_token, 0, unroll=2)
    routed = jnp.concatenate([rout[pl.ds(s, tm, stride=2 * ns), :] for s in range(2 * ns)], axis=1)
    o_ref[...] = x_ref[...] + g_ref[0] * (routed + shared)


def _combine(lposp, w_sel, wins, pad_start, y3, x, h3, gate, s_gate, s_up, s_down, geom, layer):
    m, d = x.shape
    tm = MOE_TILE
    n_tiles = m // tm
    de = s_gate.shape[-1]
    ns = y3.shape[1]
    n_experts = pad_start.shape[0]
    n_assign = TOP_K * tm
    win_words = wins.shape[1] * wins.shape[2]
    midx = functools.partial(_mod_index, n_prompt_tiles=geom["mp"] // tm,
                             tiles_per_latent_seq=geom["ls"] // tm)
    return pl.pallas_call(
        _combine_kernel,
        grid_spec=pltpu.PrefetchScalarGridSpec(
            num_scalar_prefetch=1,
            grid=(n_tiles,),
            in_specs=[
                pl.BlockSpec(memory_space=pl.ANY),
                pl.BlockSpec(memory_space=pl.ANY),
                pl.BlockSpec(memory_space=pl.ANY),
                pl.BlockSpec(memory_space=pl.ANY),
                pl.BlockSpec((tm, d), lambda i, ps: (i, 0)),
                pl.BlockSpec((tm * ns, LANES), lambda i, ps: (i, 0)),
                pl.BlockSpec((1, 1, d), lambda i, ps: (midx(i), 0, 0)),
                pl.BlockSpec((None, d, de), lambda i, ps: (layer, 0, 0)),
                pl.BlockSpec((None, d, de), lambda i, ps: (layer, 0, 0)),
                pl.BlockSpec((None, de, d), lambda i, ps: (layer, 0, 0)),
            ],
            out_specs=pl.BlockSpec((tm, d), lambda i, ps: (i, 0)),
            scratch_shapes=[
                pltpu.SMEM((n_assign,), I32),
                pltpu.SMEM((n_assign,), F32),
                pltpu.SMEM((win_words,), I32),
                pltpu.VMEM((n_assign + n_experts * (DMA_WINDOW - 1), ns, LANES), U32),
                pltpu.VMEM((tm * 2 * ns, LANES), F32),
                pltpu.SemaphoreType.DMA((3,)),
                pltpu.SemaphoreType.DMA,
            ],
        ),
        out_shape=jax.ShapeDtypeStruct((m, d), F32),
        compiler_params=_cparams("arbitrary"),
        name="moe_combine",
    )(pad_start, lposp.reshape(n_tiles, n_assign), w_sel.reshape(n_tiles, n_assign),
      wins.reshape(n_tiles, win_words), y3, x, h3.reshape(m * ns, LANES), gate, s_gate, s_up, s_down)


def _moe_layer(x, norm_g, sc, sh, gate, w_router, bias, w_gate, w_up, w_down, s_gate, s_up, s_down,
               geom, layer):
    m, d = x.shape
    n_experts = w_router.shape[1]
    tm = EXPERT_TILE
    spare = DMA_WINDOW - 1
    h3, lposd, lposp, w_sel, wins, tile_cnt, tile_base = _route(x, norm_g, sc, sh, w_router, bias, geom)

    counts = (tile_base[-1, 0] + tile_cnt[-1, 0]).astype(I32)
    padded = jnp.where(counts > 0, (counts + spare + tm - 1) // tm * tm, 0)
    pad_end = jnp.cumsum(padded)
    pad_start = pad_end - padded
    n_blocks = -(-(m * TOP_K + n_experts * (spare + tm - 1)) // tm)
    n_active = pad_end[-1] // tm
    def expert_at(slot_rows):
        return jnp.minimum(jnp.sum((pad_end[None, :] <= slot_rows[:, None]).astype(I32), axis=1), n_experts - 1)

    blocks = jnp.arange(n_blocks, dtype=I32)
    block_expert = expert_at(blocks * tm)
    prev_expert = jnp.concatenate([jnp.full((1,), -1, I32), block_expert[:-1]])
    first = (block_expert != prev_expert).astype(I32)
    parity = (jnp.cumsum(first) - 1) % 2
    run_end = jnp.sum(jnp.where(block_expert[:, None] == jnp.arange(n_experts, dtype=I32)[None, :],
                                pad_end[None, :], 0), axis=1)
    next_expert = jnp.where(run_end // tm < n_active, expert_at(run_end), -1)
    sched = (block_expert, first, next_expert.astype(I32), parity.astype(I32), n_active.astype(I32).reshape(1))
    pad_start = pad_start.astype(I32)

    xs3 = _dispatch(h3, lposd, wins, pad_start + counts, padded - counts, pad_start, n_blocks * tm)
    y3 = _experts(xs3, sched, w_gate, w_up, w_down, layer)
    return _combine(lposp, w_sel, wins, pad_start, y3, x, h3, gate, s_gate.astype(BF16), s_up.astype(BF16),
                    s_down.astype(BF16), geom, layer)


def kernel(x_prompt, x_sample, c, state_ssm_re, state_ssm_im, cache_k, cache_v, c_ctx, ada_w, ada_b, norm1_g, norm2_g, final_norm_g, s5_lam_re, s5_lam_im, s5_log_step, s5_b_re, s5_b_im, s5_c_re, s5_c_im, s5_d, s5_w_glu, attn_w_qkv, attn_b_qkv, attn_w_o, attn_sink, moe_w_router, moe_router_bias, moe_w_gate, moe_w_up, moe_w_down, moe_shared_gate, moe_shared_up, moe_shared_down):
    bp, lp, d = x_prompt.shape
    bs, ls, _ = x_sample.shape
    depth = ada_w.shape[0]
    n_kv = cache_k.shape[3]
    mp, ms = bp * lp, bs * ls
    geom = dict(bp=bp, lp=lp, bs=bs, ls=ls, mp=mp, d=d)
    assert lp % ROW_TILE == 0 and ls % ROW_TILE == 0 and bs in (4,) and bp % SUBLANES == 0
    assert (2 * n_kv * HEAD_DIM) <= d and d % (2 * n_kv * HEAD_DIM) == 0

    x = jnp.concatenate([x_prompt.reshape(mp, d), x_sample.reshape(ms, d)], axis=0).astype(F32)

    n_mod = -(-(bs + 1) // SUBLANES) * SUBLANES
    cvec = jnp.zeros((n_mod, d), F32).at[0].set(c_ctx.astype(F32)).at[1:bs + 1].set(c.astype(F32))
    mods = _ada_params(cvec, ada_w, ada_b).reshape(depth, n_mod, 6, 1, d)

    new_re, new_im, new_k, new_v = [], [], [], []
    for i in range(depth):
        j = i // N_MIXERS
        sh1, sc1, g1, sh2, sc2, g2 = (mods[i, :, part] for part in range(6))
        if i % N_MIXERS == 0:
            h = _norm_mod_call(x, norm1_g[i], sc1, sh1, geom, F32)
            params = (s5_lam_re[j], s5_lam_im[j], s5_log_step[j], s5_b_re[j], s5_b_im[j],
                      s5_c_re[j], s5_c_im[j], s5_d[j])
            g, st_re, st_im = _s5_mixer(h, state_ssm_re[:, j], state_ssm_im[:, j], params, geom)
            new_re.append(st_re)
            new_im.append(st_im)
            x = _mm_resid(g, s5_w_glu[j].astype(BF16), x, g1, geom, glu=True)
        else:
            h = _norm_mod_call(x, norm1_g[i], sc1, sh1, geom, BF16)
            qkv = _mm_bias(h, attn_w_qkv[j].astype(BF16), attn_b_qkv[j].astype(F32))
            sink = attn_sink[j].astype(F32)
            o = jnp.concatenate([
                _attn_prompt(qkv, sink, geom, n_kv),
                _attn_latent(qkv, cache_k[:, j], cache_v[:, j], sink, geom, n_kv)], axis=0)
            kw = n_kv * HEAD_DIM
            new_k.append(qkv[:mp, d:d + kw].reshape(bp, lp, n_kv, HEAD_DIM))
            new_v.append(qkv[:mp, d + kw:d + 2 * kw].reshape(bp, lp, n_kv, HEAD_DIM))
            x = _mm_resid(o, attn_w_o[j].astype(BF16), x, g1, geom, glu=False)
        x = _moe_layer(x, norm2_g[i], sc2, sh2, g2, moe_w_router[i], moe_router_bias[i],
                       moe_w_gate, moe_w_up, moe_w_down, moe_shared_gate, moe_shared_up, moe_shared_down,
                       geom, i)

    return (_final_norm(x, final_norm_g, 0, mp).reshape(bp, lp, d),
            _final_norm(x, final_norm_g, mp, ms).reshape(bs, ls, d),
            jnp.stack(new_re, axis=1), jnp.stack(new_im, axis=1),
            jnp.stack(new_k, axis=1), jnp.stack(new_v, axis=1))
```

```python
import functools
import math

import jax
import jax.numpy as jnp
from jax import lax
from jax.experimental import pallas as pl
from jax.experimental.pallas import tpu as pltpu

F32 = jnp.float32
BF16 = jnp.bfloat16
I32 = jnp.int32

GROUP_CH = 16
HEAD_DIM = 64
WINDOW = 128
ATTN_BLOCK = 128
GRID_W = 64
ROPE_BASE = 10000.0
TOP_K = 8
N_EXPERT_GROUPS = 8
TOPK_GROUPS = 4
ROUTED_SCALE = 2.5
RMS_EPS = 1e-6
NEG_INF = -1e30
N_MIXERS = 2

LANES = 128
SUBLANES = 8
VMEM_LIMIT_BYTES = 56 * 1024 * 1024

S5_CHUNK = 16
ROW_TILE = 256
EXPERT_TILE = 512
MOE_TILE = 256
DMA_WINDOW = 8
U32 = jnp.uint32

_NT = (((1,), (1,)), ((), ()))


def _cparams(*sem):
    return pltpu.CompilerParams(dimension_semantics=sem, vmem_limit_bytes=VMEM_LIMIT_BYTES)


def _dot(a, b):
    return jnp.dot(a, b, preferred_element_type=F32)


def _split(x):
    hi = x.astype(BF16)
    lo = (x - hi.astype(F32)).astype(BF16)
    return hi, lo


def _dot3(a, b):
    a_hi, a_lo = _split(a)
    b_hi, b_lo = _split(b)
    return _dot(a_hi, b_hi) + _dot(a_lo, b_hi) + _dot(a_hi, b_lo)


def _sigmoid(x):
    return 1.0 / (1.0 + jnp.exp(-x))


def _gelu_tanh(x):
    c = math.sqrt(2.0 / math.pi)
    return 0.5 * x * (1.0 + jnp.tanh(c * (x + 0.044715 * (x * x * x))))


def _mod_index(tile, n_prompt_tiles, tiles_per_latent_seq):
    return jnp.where(tile < n_prompt_tiles, 0, 1 + (tile - n_prompt_tiles) // tiles_per_latent_seq)


def _ada_kernel(c_ref, w_ref, b_ref, o_ref):
    c = c_ref[...]
    o_ref[0] = _dot3(c * _sigmoid(c), w_ref[0]) + b_ref[0]


def _ada_params(cvec, ada_w, ada_b):
    depth, d, n = ada_w.shape
    rows = cvec.shape[0]
    tn = min(512, n)
    return pl.pallas_call(
        _ada_kernel,
        grid=(depth, n // tn),
        in_specs=[
            pl.BlockSpec((rows, d), lambda l, j: (0, 0)),
            pl.BlockSpec((1, d, tn), lambda l, j: (l, 0, j)),
            pl.BlockSpec((1, 1, tn), lambda l, j: (l, 0, j)),
        ],
        out_specs=pl.BlockSpec((1, rows, tn), lambda l, j: (l, 0, j)),
        out_shape=jax.ShapeDtypeStruct((depth, rows, n), F32),
        compiler_params=_cparams("arbitrary", "arbitrary"),
        name="ada_params",
    )(cvec, ada_w, ada_b.reshape(depth, 1, n))


def _norm_mod(x, g, sc, sh):
    ms = jnp.mean(x * x, axis=-1, keepdims=True)
    return (x * lax.rsqrt(ms + RMS_EPS) * g) * (1.0 + sc) + sh


def _norm_mod_kernel(x_ref, g_ref, sc_ref, sh_ref, o_ref):
    o_ref[...] = _norm_mod(x_ref[...], g_ref[...], sc_ref[0], sh_ref[0]).astype(o_ref.dtype)


def _norm_mod_call(x, g, sc, sh, geom, out_dtype):
    m, d = x.shape
    tm = ROW_TILE
    midx = functools.partial(_mod_index, n_prompt_tiles=geom["mp"] // tm,
                             tiles_per_latent_seq=geom["ls"] // tm)
    return pl.pallas_call(
        _norm_mod_kernel,
        grid=(m // tm,),
        in_specs=[
            pl.BlockSpec((tm, d), lambda i: (i, 0)),
            pl.BlockSpec((1, d), lambda i: (0, 0)),
            pl.BlockSpec((1, 1, d), lambda i: (midx(i), 0, 0)),
            pl.BlockSpec((1, 1, d), lambda i: (midx(i), 0, 0)),
        ],
        out_specs=pl.BlockSpec((tm, d), lambda i: (i, 0)),
        out_shape=jax.ShapeDtypeStruct((m, d), out_dtype),
        compiler_params=_cparams("arbitrary"),
        name="norm_mod",
    )(x, g.reshape(1, d), sc, sh)


def _final_norm_kernel(x_ref, g_ref, o_ref):
    x = x_ref[...]
    ms = jnp.mean(x * x, axis=-1, keepdims=True)
    o_ref[...] = x * lax.rsqrt(ms + RMS_EPS) * g_ref[...]


def _final_norm(x, g, row0, n_rows):
    d = x.shape[1]
    tm = ROW_TILE
    return pl.pallas_call(
        _final_norm_kernel,
        grid=(n_rows // tm,),
        in_specs=[
            pl.BlockSpec((tm, d), lambda i: (row0 // tm + i, 0)),
            pl.BlockSpec((1, d), lambda i: (0, 0)),
        ],
        out_specs=pl.BlockSpec((tm, d), lambda i: (i, 0)),
        out_shape=jax.ShapeDtypeStruct((n_rows, d), F32),
        compiler_params=_cparams("arbitrary"),
        name="final_norm",
    )(x, g.astype(F32).reshape(1, d))


def _mm_bias_kernel(x_ref, w_ref, b_ref, o_ref):
    o_ref[...] = _dot(x_ref[...], w_ref[...]) + b_ref[...]


def _mm_bias(x, w, b):
    m, k = x.shape
    n = w.shape[1]
    tm = min(512, m)
    tn = n // 2 if (n // 2) % LANES == 0 else min(512, n)
    return pl.pallas_call(
        _mm_bias_kernel,
        grid=(n // tn, m // tm),
        in_specs=[
            pl.BlockSpec((tm, k), lambda j, i: (i, 0)),
            pl.BlockSpec((k, tn), lambda j, i: (0, j)),
            pl.BlockSpec((1, tn), lambda j, i: (0, j)),
        ],
        out_specs=pl.BlockSpec((tm, tn), lambda j, i: (i, j)),
        out_shape=jax.ShapeDtypeStruct((m, n), F32),
        compiler_params=_cparams("arbitrary", "arbitrary"),
        name="mm_bias",
    )(x, w, b.reshape(1, n))


def _mm_resid_kernel(a_ref, w_ref, x_ref, g_ref, o_ref):
    o_ref[...] = x_ref[...] + g_ref[0] * _dot(a_ref[...], w_ref[...])


def _mm_glu_resid_kernel(a_ref, wv_ref, wg_ref, x_ref, g_ref, o_ref):
    a = a_ref[...].astype(BF16)
    val = _dot(a, wv_ref[...])
    gate = _dot(a, wg_ref[...])
    o_ref[...] = x_ref[...] + g_ref[0] * (val * _sigmoid(gate))


def _mm_resid(a, w, x, gate, geom, glu):
    m, k = a.shape
    n = x.shape[1]
    tm = math.gcd(geom["mp"], geom["ls"], 2 * ROW_TILE)
    tn = min(1024, n)
    nj = n // tn
    midx = functools.partial(_mod_index, n_prompt_tiles=geom["mp"] // tm,
                             tiles_per_latent_seq=geom["ls"] // tm)
    a_spec = pl.BlockSpec((tm, k), lambda j, i: (i, 0))
    w_spec = pl.BlockSpec((k, tn), lambda j, i: (0, j))
    x_spec = pl.BlockSpec((tm, tn), lambda j, i: (i, j))
    g_spec = pl.BlockSpec((1, 1, tn), lambda j, i: (midx(i), 0, j))
    if glu:
        body = _mm_glu_resid_kernel
        in_specs = [a_spec, w_spec, pl.BlockSpec((k, tn), lambda j, i: (0, j + nj)), x_spec, g_spec]
        args = (a, w, w, x, gate)
    else:
        body = _mm_resid_kernel
        in_specs = [a_spec, w_spec, x_spec, g_spec]
        args = (a, w, x, gate)
    return pl.pallas_call(
        body,
        grid=(nj, m // tm),
        in_specs=in_specs,
        out_specs=pl.BlockSpec((tm, tn), lambda j, i: (i, j)),
        out_shape=jax.ShapeDtypeStruct((m, n), F32),
        compiler_params=_cparams("arbitrary", "arbitrary"),
        name="mm_glu_resid" if glu else "mm_resid",
    )(*args)


def _s5_operands(lam_re, lam_im, log_step, b_re, b_im, c_re, c_im, d_skip):
    t = S5_CHUNK
    n_groups, n_state = lam_re.shape[1:]
    ch = GROUP_CH
    tc = t * ch
    lam = lax.complex(jnp.minimum(lam_re.astype(F32), -1e-4), lam_im.astype(F32))
    step = jnp.exp(log_step.astype(F32))[..., None]
    lam_bar = jnp.exp(lam * step)
    b_bar = ((lam_bar - 1.0) / lam)[..., None] * lax.complex(b_re.astype(F32), b_im.astype(F32))
    c_mat = lax.complex(c_re.astype(F32), c_im.astype(F32))
    pw = [jnp.ones_like(lam_bar)]
    for _ in range(t):
        pw.append(pw[-1] * lam_bar)
    pw = jnp.stack(pw, axis=1)

    w1_f = jnp.einsum("tgp,gph->gthp", pw[0, t - 1::-1][:t], b_bar[0]).reshape(n_groups, tc, n_state)
    w1_b = jnp.einsum("tgp,gph->gthp", pw[1, :t], b_bar[1]).reshape(n_groups, tc, n_state)
    zeros = jnp.zeros((n_groups, tc, n_state), F32)

    def pair_cols(m, gi):
        return jnp.concatenate([m, zeros] if gi == 0 else [zeros, m], axis=-1)

    def w1_for(gi):
        sel = slice(gi, None, 2)
        return jnp.concatenate([
            pair_cols(jnp.real(w1_f), gi)[sel], pair_cols(jnp.imag(w1_f), gi)[sel],
            pair_cols(jnp.real(w1_b), gi)[sel], pair_cols(jnp.imag(w1_b), gi)[sel]], axis=-1)

    w1 = jnp.stack([w1_for(0), w1_for(1)], axis=1)

    k_f = jnp.real(jnp.einsum("gop,kgp,gpi->gkio", c_mat[0], pw[0, :t], b_bar[0]))
    k_b = jnp.real(jnp.einsum("gop,kgp,gpi->gkio", c_mat[1], pw[1, :t], b_bar[1]))
    s_idx = jnp.arange(t)[:, None]
    t_idx = jnp.arange(t)[None, :]
    lag_f = t_idx - s_idx
    lag_b = s_idx - t_idx
    toep_f = jnp.where((lag_f >= 0)[None, :, :, None, None], k_f[:, jnp.clip(lag_f, 0, t - 1)], 0.0)
    toep_b = jnp.where((lag_b >= 0)[None, :, :, None, None], k_b[:, jnp.clip(lag_b, 0, t - 1)], 0.0)
    eye_t = jnp.eye(t, dtype=F32)[None, :, :, None, None]
    skip = eye_t * (d_skip.astype(F32).reshape(n_groups, ch)[:, None, None, :, None]
                    * jnp.eye(ch, dtype=F32)[None, None, None])
    toep = (toep_f + toep_b + skip).transpose(0, 1, 3, 2, 4).reshape(n_groups, tc, tc)
    cf = jnp.einsum("gop,tgp->gpto", c_mat[0], pw[0, 1:t + 1]).reshape(n_groups, n_state, tc)
    cb = jnp.einsum("gop,tgp->gpto", c_mat[1], pw[1, t:0:-1]).reshape(n_groups, n_state, tc)
    zrow = jnp.zeros((n_groups, n_state, tc), F32)

    def pair_rows(m):
        even = jnp.concatenate([m, zrow], axis=1)
        odd = jnp.concatenate([zrow, m], axis=1)
        return jnp.where((jnp.arange(n_groups) % 2 == 0)[:, None, None], even, odd)

    w3 = jnp.concatenate([toep, pair_rows(jnp.real(cf)), pair_rows(-jnp.imag(cf)),
                          pair_rows(jnp.real(cb)), pair_rows(-jnp.imag(cb))], axis=1).astype(BF16)

    a_pow = pw[:, t]
    a_vec = jnp.stack([jnp.real(a_pow[0]), jnp.imag(a_pow[0]), jnp.real(a_pow[1]), jnp.imag(a_pow[1])],
                      axis=1).reshape(n_groups // 2, 2, 4, n_state)
    a_vec = a_vec.transpose(0, 2, 1, 3).reshape(n_groups // 2, 4, 2 * n_state)
    return w1, w3, a_vec


def _complex_step(a_re, a_im, h_re, h_im, s_re, s_im):
    return a_re * h_re - a_im * h_im + s_re, a_re * h_im + a_im * h_re + s_im


def _s5_scan(s_scr, h_scr, a, init, n_chunks, rows_per_chunk):
    pl_ = LANES
    if rows_per_chunk % SUBLANES == 0:
        rt = rows_per_chunk

        def body(c, carry):
            f_re, f_im, g_re, g_im = carry
            rf = pl.ds(pl.multiple_of(c * rt, SUBLANES), rt)
            rb = pl.ds(pl.multiple_of((n_chunks - 1 - c) * rt, SUBLANES), rt)
            h_scr[0, rf, :] = f_re
            h_scr[1, rf, :] = f_im
            h_scr[2, rb, :] = g_re
            h_scr[3, rb, :] = g_im
            f_re, f_im = _complex_step(a[0:1], a[1:2], f_re, f_im,
                                       s_scr[rf, 0:pl_], s_scr[rf, pl_:2 * pl_])
            g_re, g_im = _complex_step(a[2:3], a[3:4], g_re, g_im,
                                       s_scr[rb, 2 * pl_:3 * pl_], s_scr[rb, 3 * pl_:4 * pl_])
            return f_re, f_im, g_re, g_im

        return lax.fori_loop(0, n_chunks, body, tuple(init))

    assert rows_per_chunk * 2 == SUBLANES and n_chunks % 2 == 0
    half = rows_per_chunk
    n_tiles = n_chunks // 2
    low = lax.broadcasted_iota(I32, (SUBLANES, pl_), 0) < half

    def body(j, carry):
        f_re, f_im, g_re, g_im = carry
        rf = pl.ds(pl.multiple_of(j * SUBLANES, SUBLANES), SUBLANES)
        rb = pl.ds(pl.multiple_of((n_tiles - 1 - j) * SUBLANES, SUBLANES), SUBLANES)
        sf_re, sf_im = s_scr[rf, 0:pl_], s_scr[rf, pl_:2 * pl_]
        sb_re, sb_im = s_scr[rb, 2 * pl_:3 * pl_], s_scr[rb, 3 * pl_:4 * pl_]
        f1_re, f1_im = _complex_step(a[0:1], a[1:2], f_re, f_im, sf_re, sf_im)
        g1_re, g1_im = _complex_step(a[2:3], a[3:4], g_re, g_im, sb_re, sb_im)
        f1s_re, f1s_im = pltpu.roll(f1_re, half, 0), pltpu.roll(f1_im, half, 0)
        g1s_re, g1s_im = pltpu.roll(g1_re, half, 0), pltpu.roll(g1_im, half, 0)
        h_scr[0, rf, :] = jnp.where(low, f_re, f1s_re)
        h_scr[1, rf, :] = jnp.where(low, f_im, f1s_im)
        h_scr[2, rb, :] = jnp.where(low, g1s_re, g_re)
        h_scr[3, rb, :] = jnp.where(low, g1s_im, g_im)
        f2_re, f2_im = _complex_step(a[0:1], a[1:2], f1s_re, f1s_im, sf_re, sf_im)
        g2_re, g2_im = _complex_step(a[2:3], a[3:4], g1s_re, g1s_im, sb_re, sb_im)
        return (pltpu.roll(f2_re, half, 0), pltpu.roll(f2_im, half, 0),
                pltpu.roll(g2_re, half, 0), pltpu.roll(g2_im, half, 0))

    return lax.fori_loop(0, n_tiles, body, tuple(init))


def _s5_kernel(up_ref, us_ref, w1_ref, w3_ref, a_ref, h0_ref, gp_ref, gs_ref, fin_ref,
               sp_scr, ss_scr, hp_scr, hs_scr, *, geom):
    tc = up_ref.shape[-1]
    a = a_ref[0]
    w1 = [w1_ref[0, 0], w1_ref[0, 1]]
    w1_hi = [w.astype(BF16) for w in w1]

    sp_scr[...] = _dot3(up_ref[0], w1[0]) + _dot3(up_ref[1], w1[1])
    ss_scr[...] = _dot(us_ref[0].astype(BF16), w1_hi[0]) + _dot(us_ref[1].astype(BF16), w1_hi[1])

    zero = jnp.zeros((geom["bp"], LANES), F32)
    fin = _s5_scan(sp_scr, hp_scr, a, (zero, zero, zero, zero), geom["lp"] // S5_CHUNK, geom["bp"])
    for comp in range(4):
        fin_ref[0, comp] = fin[comp]
    _s5_scan(ss_scr, hs_scr, a, tuple(h0_ref[0, comp] for comp in range(4)),
             geom["ls"] // S5_CHUNK, geom["bs"])

    for u_ref, h_scr, o_ref in ((up_ref, hp_scr, gp_ref), (us_ref, hs_scr, gs_ref)):
        for gi in range(2):
            y = _dot(u_ref[gi].astype(BF16), w3_ref[gi, 0:tc, :])
            for comp in range(4):
                y += _dot(h_scr[comp].astype(BF16),
                          w3_ref[gi, tc + comp * LANES:tc + (comp + 1) * LANES, :])
            o_ref[gi] = _gelu_tanh(y).astype(o_ref.dtype)


def _s5_mixer(h, h0_re, h0_im, params, geom):
    m, d = h.shape
    t = S5_CHUNK
    ch = GROUP_CH
    n_groups = d // ch
    tc = t * ch
    bp, lp, bs, ls, mp = geom["bp"], geom["lp"], geom["bs"], geom["ls"], geom["mp"]
    rp, rs = mp // t, (m - mp) // t
    w1, w3, a_vec = _s5_operands(*params)
    n_state = a_vec.shape[-1] // 2

    def to_chunks(x, b, l):
        x = x.reshape(b, l // t, t, n_groups, ch).transpose(3, 1, 0, 2, 4)
        return x.reshape(n_groups, (l // t) * b, tc)

    def from_chunks(x, b, l):
        x = x.reshape(n_groups, l // t, b, t, ch).transpose(2, 1, 3, 0, 4)
        return x.reshape(b * l, d)

    up = to_chunks(h[:mp], bp, lp)
    us = to_chunks(h[mp:], bs, ls)
    h0 = jnp.stack([h0_re[:, 0], h0_im[:, 0], h0_re[:, 1], h0_im[:, 1]], axis=0).astype(F32)
    h0 = h0.reshape(4, bs, n_groups // 2, 2 * n_state).transpose(2, 0, 1, 3)
    reps = max(1, SUBLANES // bs)
    h0 = jnp.tile(h0, (1, 1, reps, 1))
    h0_rows = h0.shape[2]

    gp, gs, fin = pl.pallas_call(
        functools.partial(_s5_kernel, geom=geom),
        grid=(n_groups // 2,),
        in_specs=[
            pl.BlockSpec((2, rp, tc), lambda i: (i, 0, 0)),
            pl.BlockSpec((2, rs, tc), lambda i: (i, 0, 0)),
            pl.BlockSpec((1, 2, tc, 4 * LANES), lambda i: (i, 0, 0, 0)),
            pl.BlockSpec((2, tc + 4 * LANES, tc), lambda i: (i, 0, 0)),
            pl.BlockSpec((1, 4, LANES), lambda i: (i, 0, 0)),
            pl.BlockSpec((1, 4, h0_rows, LANES), lambda i: (i, 0, 0, 0)),
        ],
        out_specs=[
            pl.BlockSpec((2, rp, tc), lambda i: (i, 0, 0)),
            pl.BlockSpec((2, rs, tc), lambda i: (i, 0, 0)),
            pl.BlockSpec((1, 4, bp, LANES), lambda i: (i, 0, 0, 0)),
        ],
        out_shape=[
            jax.ShapeDtypeStruct((n_groups, rp, tc), F32),
            jax.ShapeDtypeStruct((n_groups, rs, tc), F32),
            jax.ShapeDtypeStruct((n_groups // 2, 4, bp, LANES), F32),
        ],
        scratch_shapes=[
            pltpu.VMEM((rp, 4 * LANES), F32),
            pltpu.VMEM((rs, 4 * LANES), F32),
            pltpu.VMEM((4, rp, LANES), F32),
            pltpu.VMEM((4, rs, LANES), F32),
        ],
        compiler_params=_cparams("arbitrary"),
        name="s5_mixer",
    )(up, us, w1, w3, a_vec, h0)

    g = jnp.concatenate([from_chunks(gp, bp, lp), from_chunks(gs, bs, ls)], axis=0)
    fin = fin.reshape(n_groups // 2, 2, 2, bp, 2, n_state).transpose(3, 1, 2, 0, 4, 5)
    fin = fin.reshape(bp, 2, 2, n_groups, n_state)
    return g, fin[:, :, 0], fin[:, :, 1]


def _softmax_with_sink(scores, sink):
    mx = sink
    for s in scores:
        mx = jnp.maximum(mx, jnp.max(s, axis=-1, keepdims=True))
    ps = [jnp.exp(s - mx) for s in scores]
    den = jnp.exp(sink - mx)
    for p in ps:
        den = den + jnp.sum(p, axis=-1, keepdims=True)
    return ps, 1.0 / den


def _attn_prompt_kernel(sink_ref, q_ref, kv_ref, o_ref, *, n_kv, q_per_kv):
    hd = HEAD_DIM
    scale = hd ** -0.5
    for kv in range(n_kv):
        k = kv_ref[:, kv * hd:(kv + 1) * hd].astype(BF16)
        v = kv_ref[:, (n_kv + kv) * hd:(n_kv + kv + 1) * hd].astype(BF16)
        for gq in range(q_per_kv):
            head = kv * q_per_kv + gq
            q = (q_ref[:, head * hd:(head + 1) * hd] * scale).astype(BF16)
            s = lax.dot_general(q, k, _NT, preferred_element_type=F32)
            (p,), inv = _softmax_with_sink([s], sink_ref[head])
            o_ref[:, head * hd:(head + 1) * hd] = (_dot(p.astype(BF16), v) * inv).astype(o_ref.dtype)


def _attn_prompt(qkv, sink, geom, n_kv):
    d = geom["d"]
    bp, lp = geom["bp"], geom["lp"]
    kvw = 2 * n_kv * HEAD_DIM
    n_heads = d // HEAD_DIM
    return pl.pallas_call(
        functools.partial(_attn_prompt_kernel, n_kv=n_kv, q_per_kv=n_heads // n_kv),
        grid=(bp,),
        in_specs=[
            pl.BlockSpec(memory_space=pltpu.SMEM),
            pl.BlockSpec((lp, d), lambda b: (b, 0)),
            pl.BlockSpec((lp, kvw), lambda b: (b, d // kvw)),
        ],
        out_specs=pl.BlockSpec((lp, d), lambda b: (b, 0)),
        out_shape=jax.ShapeDtypeStruct((bp * lp, d), BF16),
        compiler_params=_cparams("arbitrary"),
        name="attn_prompt",
    )(sink, qkv, qkv)


def _rope_tables(length):
    pairs = HEAD_DIM // 4
    pos = jnp.arange(length)
    row = (pos // GRID_W).astype(F32)
    col = (pos % GRID_W).astype(F32)
    inv_freq = ROPE_BASE ** (-jnp.arange(pairs, dtype=F32) / pairs)
    lane = jnp.arange(LANES) % HEAD_DIM
    axis_pos = jnp.where((lane // (2 * pairs))[None, :] == 0, row[:, None], col[:, None])
    ang = axis_pos * inv_freq[lane % pairs][None, :]
    sign = jnp.where((lane % (2 * pairs)) < pairs, -1.0, 1.0)[None, :]
    return jnp.cos(ang), jnp.sin(ang) * sign


def _rope(x, cos, sin):
    w = x.shape[1]
    pairs = HEAD_DIM // 4
    reps = w // LANES
    cos_w = jnp.concatenate([cos] * reps, axis=1) if reps > 1 else cos
    sin_w = jnp.concatenate([sin] * reps, axis=1) if reps > 1 else sin
    lane = lax.broadcasted_iota(I32, x.shape, 1)
    partner = jnp.where(lane % (2 * pairs) < pairs, pltpu.roll(x, w - pairs, 1), pltpu.roll(x, pairs, 1))
    return x * cos_w + partner * sin_w


def _attn_latent_kernel(sink_ref, q_ref, kvp_ref, kvo_ref, kvn_ref, ck_ref, cv_ref,
                        cos_p, sin_p, cos_o, sin_o, cos_n, sin_n, o_ref, *, n_kv, q_per_kv, n_blocks):
    hd = HEAD_DIM
    blk = ATTN_BLOCK
    scale = hd ** -0.5
    i = pl.program_id(1)
    kw = n_kv * hd
    k_win = jnp.concatenate([
        _rope(kvp_ref[:, 0:kw], cos_p[...], sin_p[...]),
        _rope(kvo_ref[:, 0:kw], cos_o[...], sin_o[...]),
        _rope(kvn_ref[:, 0:kw], cos_n[...], sin_n[...])], axis=0).astype(BF16)
    v_win = jnp.concatenate([kvp_ref[:, kw:2 * kw], kvo_ref[:, kw:2 * kw], kvn_ref[:, kw:2 * kw]],
                            axis=0).astype(BF16)
    rows = q_per_kv * blk
    r_in_blk = lax.broadcasted_iota(I32, (rows, 3 * blk), 0) % blk
    c_idx = lax.broadcasted_iota(I32, (rows, 3 * blk), 1)
    k_pos = (i - 1) * blk + c_idx
    mask = (jnp.abs(c_idx - blk - r_in_blk) <= WINDOW) & (k_pos >= 0) & (k_pos < n_blocks * blk)
    for kv in range(n_kv):
        qg = _rope(q_ref[:, kv * q_per_kv * hd:(kv + 1) * q_per_kv * hd], cos_o[...], sin_o[...]) * scale
        q = jnp.concatenate([qg[:, gq * hd:(gq + 1) * hd] for gq in range(q_per_kv)], axis=0).astype(BF16)
        sink = jnp.concatenate(
            [jnp.full((blk, 1), sink_ref[kv * q_per_kv + gq], F32) for gq in range(q_per_kv)], axis=0)
        s_ctx = lax.dot_general(q, ck_ref[0, :, kv * hd:(kv + 1) * hd].astype(BF16), _NT,
                                preferred_element_type=F32)
        s_win = lax.dot_general(q, k_win[:, kv * hd:(kv + 1) * hd], _NT, preferred_element_type=F32)
        s_win = jnp.where(mask, s_win, NEG_INF)
        (p_ctx, p_win), inv = _softmax_with_sink([s_ctx, s_win], sink)
        o = (_dot(p_ctx.astype(BF16), cv_ref[0, :, kv * hd:(kv + 1) * hd].astype(BF16))
             + _dot(p_win.astype(BF16), v_win[:, kv * hd:(kv + 1) * hd])) * inv
        for gq in range(q_per_kv):
            head = kv * q_per_kv + gq
            o_ref[:, head * hd:(head + 1) * hd] = o[gq * blk:(gq + 1) * blk].astype(o_ref.dtype)


def _attn_latent(qkv, ctx_k, ctx_v, sink, geom, n_kv):
    d = geom["d"]
    bs, ls, mp = geom["bs"], geom["ls"], geom["mp"]
    blk = ATTN_BLOCK
    nb = ls // blk
    kvw = 2 * n_kv * HEAD_DIM
    n_heads = d // HEAD_DIM
    past = ctx_k.shape[1]
    ck = ctx_k.reshape(bs, past, n_kv * HEAD_DIM).astype(F32)
    cv = ctx_v.reshape(bs, past, n_kv * HEAD_DIM).astype(F32)
    cos, sin = _rope_tables(ls)
    base = mp // blk

    def prev(i):
        return jnp.maximum(i - 1, 0)

    def nxt(i):
        return jnp.minimum(i + 1, nb - 1)

    def kv_spec(f):
        return pl.BlockSpec((blk, kvw), lambda b, i: (base + b * nb + f(i), d // kvw))

    def tab_spec(f):
        return pl.BlockSpec((blk, LANES), lambda b, i: (f(i), 0))

    same = lambda i: i
    return pl.pallas_call(
        functools.partial(_attn_latent_kernel, n_kv=n_kv, q_per_kv=n_heads // n_kv, n_blocks=nb),
        grid=(bs, nb),
        in_specs=[
            pl.BlockSpec(memory_space=pltpu.SMEM),
            pl.BlockSpec((blk, d), lambda b, i: (base + b * nb + i, 0)),
            kv_spec(prev), kv_spec(same), kv_spec(nxt),
            pl.BlockSpec((1, past, n_kv * HEAD_DIM), lambda b, i: (b, 0, 0)),
            pl.BlockSpec((1, past, n_kv * HEAD_DIM), lambda b, i: (b, 0, 0)),
            tab_spec(prev), tab_spec(prev), tab_spec(same), tab_spec(same), tab_spec(nxt), tab_spec(nxt),
        ],
        out_specs=pl.BlockSpec((blk, d), lambda b, i: (b * nb + i, 0)),
        out_shape=jax.ShapeDtypeStruct((bs * ls, d), BF16),
        compiler_params=_cparams("arbitrary", "arbitrary"),
        name="attn_latent",
    )(sink, qkv, qkv, qkv, qkv, ck, cv, cos, sin, cos, sin, cos, sin)


def _pack_bf16_pairs(x):
    n = x.shape[1] // 2

    def rounded_bits(v):
        return lax.bitcast_convert_type(v, U32) + jnp.uint32(0x8000)

    return (rounded_bits(x[:, :n]) & jnp.uint32(0xFFFF0000)) | (rounded_bits(x[:, n:]) >> 16)


def _store_row_tiles(ref, words):
    for s in range(ref.shape[1]):
        ref[:, s, :] = words[:, s * LANES:(s + 1) * LANES]


def _unpack_hi(words):
    return lax.bitcast_convert_type(words & jnp.uint32(0xFFFF0000), F32)


def _unpack_lo(words):
    return lax.bitcast_convert_type(words << 16, F32)


def _load_row_tiles_bf16(ref, ns):
    rows = ref.shape[0] // ns
    tiles = [ref[pl.ds(s, rows, stride=ns), :] for s in range(ns)]
    return jnp.concatenate([_unpack_hi(t).astype(BF16) for t in tiles]
                           + [_unpack_lo(t).astype(BF16) for t in tiles], axis=1)


def _router_kernel(x_ref, g_ref, sc_ref, sh_ref, wt_hi_ref, wt_lo_ref, bias_ref,
                   h3_ref, lposd_ref, lposp_ref, w_ref, win_ref, cnt_ref, base_ref, base_scr, base_col_scr,
                   *, n_experts):
    step = pl.program_id(0)

    @pl.when(step == 0)
    def _():
        base_scr[...] = jnp.zeros_like(base_scr)
        base_col_scr[...] = jnp.zeros_like(base_col_scr)

    h = _norm_mod(x_ref[...], g_ref[...], sc_ref[0], sh_ref[0])
    _store_row_tiles(h3_ref, _pack_bf16_pairs(h))
    tm = h.shape[0]
    h_hi, h_lo = _split(h)
    wt_hi = wt_hi_ref[...]
    logits = (lax.dot_general(wt_hi, h_hi, _NT, preferred_element_type=F32)
              + lax.dot_general(wt_lo_ref[...], h_hi, _NT, preferred_element_type=F32)
              + lax.dot_general(wt_hi, h_lo, _NT, preferred_element_type=F32))
    scores = _sigmoid(logits)
    sel = scores + bias_ref[:, 0:1]
    per_group = n_experts // N_EXPERT_GROUPS
    e_iota = lax.broadcasted_iota(I32, (n_experts, tm), 0)
    big = jnp.int32(1 << 30)

    grp = []
    for gidx in range(N_EXPERT_GROUPS):
        v = sel[gidx * per_group:(gidx + 1) * per_group]
        r = lax.broadcasted_iota(I32, v.shape, 0) + gidx * per_group
        m1 = jnp.max(v, axis=0, keepdims=True)
        i1 = jnp.min(jnp.where(v == m1, r, big), axis=0, keepdims=True)
        m2 = jnp.max(jnp.where(r == i1, -jnp.inf, v), axis=0, keepdims=True)
        grp.append(jnp.broadcast_to(m1 + m2, v.shape))
    cur = jnp.concatenate(grp, axis=0)
    g_iota = e_iota // per_group
    cand = jnp.full(sel.shape, -jnp.inf, F32)
    for _ in range(TOPK_GROUPS):
        mx = jnp.max(cur, axis=0, keepdims=True)
        gi = jnp.min(jnp.where(cur == mx, g_iota, big), axis=0, keepdims=True)
        hit = g_iota == gi
        cand = jnp.where(hit, sel, cand)
        cur = jnp.where(hit, -jnp.inf, cur)

    idxs, wts = [], []
    onehot = jnp.zeros((n_experts, tm), F32)
    for _ in range(TOP_K):
        mx = jnp.max(cand, axis=0, keepdims=True)
        ei = jnp.min(jnp.where(cand == mx, e_iota, big), axis=0, keepdims=True)
        hit = e_iota == ei
        idxs.append(ei)
        wts.append(jnp.sum(jnp.where(hit, scores, 0.0), axis=0, keepdims=True))
        onehot = jnp.where(hit, 1.0, onehot)
        cand = jnp.where(hit, -jnp.inf, cand)
    wsum = wts[0]
    for wk in wts[1:]:
        wsum = wsum + wk
    w_ref[0] = jnp.concatenate(wts, axis=0) / wsum * ROUTED_SCALE

    onehot_b = onehot.astype(BF16)
    t_row = lax.broadcasted_iota(I32, (tm, tm), 0)
    t_col = lax.broadcasted_iota(I32, (tm, tm), 1)
    earlier_tok = _dot(onehot_b, jnp.where(t_row < t_col, 1.0, 0.0).astype(BF16))
    cnt = jnp.sum(onehot, axis=1, keepdims=True)
    cnt_pad = jnp.floor((cnt + (DMA_WINDOW - 1)) * (1.0 / DMA_WINDOW)) * DMA_WINDOW
    e_row = lax.broadcasted_iota(I32, (n_experts, n_experts), 0)
    e_col = lax.broadcasted_iota(I32, (n_experts, n_experts), 1)
    lower = jnp.where(e_col < e_row, 1.0, 0.0).astype(BF16)
    start_d = _dot(lower, jnp.broadcast_to(cnt, (n_experts, LANES)).astype(BF16))[:, 0:1]
    start_p = _dot(lower, jnp.broadcast_to(cnt_pad, (n_experts, LANES)).astype(BF16))[:, 0:1]
    pos_d = start_d + earlier_tok
    pos_p = start_p + earlier_tok
    lposd_ref[0] = jnp.concatenate(
        [jnp.sum(jnp.where(e_iota == ei, pos_d, 0.0), axis=0, keepdims=True) for ei in idxs], axis=0).astype(I32)
    lposp_ref[0] = jnp.concatenate(
        [jnp.sum(jnp.where(e_iota == ei, pos_p, 0.0), axis=0, keepdims=True) for ei in idxs], axis=0).astype(I32)

    n_win_max = win_ref.shape[2]
    nw = cnt_pad * (1.0 / DMA_WINDOW)
    w_start = _dot(lower, jnp.broadcast_to(nw, (n_experts, LANES)).astype(BF16))[:, 0:1]
    q = lax.broadcasted_iota(I32, (n_experts, n_win_max), 1).astype(F32)
    w_expert = jnp.sum(jnp.where(w_start + nw <= q, 1.0, 0.0), axis=0, keepdims=True)
    mine = lax.broadcasted_iota(I32, (n_experts, n_win_max), 0).astype(F32) == w_expert

    def pick(col):
        return jnp.sum(jnp.where(mine, col, 0.0), axis=0, keepdims=True)

    q_row = lax.broadcasted_iota(I32, (1, n_win_max), 1).astype(F32)
    offset = (q_row - pick(w_start)) * DMA_WINDOW
    total = jnp.broadcast_to(jnp.sum(nw, axis=0, keepdims=True), (1, n_win_max))
    zero_row = jnp.zeros((1, n_win_max), F32)
    win_ref[0] = jnp.concatenate(
        [w_expert, pick(base_col_scr[:, 0:1]) + offset, pick(start_d) + offset, pick(start_p) + offset, total,
         zero_row, zero_row, zero_row], axis=0).astype(I32)
    base_col_scr[...] = base_col_scr[...] + cnt

    cnt_row = lax.dot_general(jnp.ones((SUBLANES, tm), BF16), onehot_b, _NT, preferred_element_type=F32)
    cnt_ref[0] = cnt_row
    base_ref[0] = base_scr[...]
    base_scr[...] = base_scr[...] + cnt_row


def _route(x, g, sc, sh, w_router, bias, geom):
    m, d = x.shape
    n_experts = w_router.shape[1]
    tm = MOE_TILE
    n_tiles = m // tm
    ns = d // (2 * LANES)
    wt = w_router.astype(F32).T
    wt_hi = wt.astype(BF16)
    wt_lo = (wt - wt_hi.astype(F32)).astype(BF16)
    bias_b = jnp.broadcast_to(bias.astype(F32)[:, None], (n_experts, LANES))
    midx = functools.partial(_mod_index, n_prompt_tiles=geom["mp"] // tm,
                             tiles_per_latent_seq=geom["ls"] // tm)
    n_win_max = TOP_K * tm // DMA_WINDOW + n_experts
    tok_spec = pl.BlockSpec((1, TOP_K, tm), lambda i: (i, 0, 0))
    win_spec = pl.BlockSpec((1, SUBLANES, n_win_max), lambda i: (i, 0, 0))
    exp_spec = pl.BlockSpec((1, SUBLANES, n_experts), lambda i: (i, 0, 0))
    return pl.pallas_call(
        functools.partial(_router_kernel, n_experts=n_experts),
        grid=(n_tiles,),
        in_specs=[
            pl.BlockSpec((tm, d), lambda i: (i, 0)),
            pl.BlockSpec((1, d), lambda i: (0, 0)),
            pl.BlockSpec((1, 1, d), lambda i: (midx(i), 0, 0)),
            pl.BlockSpec((1, 1, d), lambda i: (midx(i), 0, 0)),
            pl.BlockSpec((n_experts, d), lambda i: (0, 0)),
            pl.BlockSpec((n_experts, d), lambda i: (0, 0)),
            pl.BlockSpec((n_experts, LANES), lambda i: (0, 0)),
        ],
        out_specs=[
            pl.BlockSpec((tm, ns, LANES), lambda i: (i, 0, 0)),
            tok_spec, tok_spec, tok_spec, win_spec, exp_spec, exp_spec,
        ],
        out_shape=[
            jax.ShapeDtypeStruct((m, ns, LANES), U32),
            jax.ShapeDtypeStruct((n_tiles, TOP_K, tm), I32),
            jax.ShapeDtypeStruct((n_tiles, TOP_K, tm), I32),
            jax.ShapeDtypeStruct((n_tiles, TOP_K, tm), F32),
            jax.ShapeDtypeStruct((n_tiles, SUBLANES, n_win_max), I32),
            jax.ShapeDtypeStruct((n_tiles, SUBLANES, n_experts), F32),
            jax.ShapeDtypeStruct((n_tiles, SUBLANES, n_experts), F32),
        ],
        scratch_shapes=[pltpu.VMEM((SUBLANES, n_experts), F32), pltpu.VMEM((n_experts, LANES), F32)],
        compiler_params=_cparams("arbitrary"),
        name="moe_router",
    )(x, g.reshape(1, d), sc, sh, wt_hi, wt_lo, bias_b)


_FILL_SIZES = tuple(1 << b for b in range(EXPERT_TILE.bit_length() - 1, -1, -1))


def _drain_windows(n_win, n_win_max, copy_of_slots):
    max_chunk = TOP_K * MOE_TILE // DMA_WINDOW
    sizes = [max_chunk] * (n_win_max // max_chunk)
    size = max_chunk // 2
    while size >= 1:
        sizes.append(size)
        size //= 2
    left = n_win
    for s in sizes:
        take = left >= s

        @pl.when(take)
        def _():
            copy_of_slots(s * DMA_WINDOW).wait()

        left = left - jnp.where(take, s, 0)


def _dispatch_kernel(fill_start_ref, fill_n_ref, pad_start_ref, lpos_hbm, win_hbm, h3_ref, xs_hbm,
                     lpos_smem, win_smem, buf, zero_scr, idx_sem, row_sem, *, n_experts):
    step = pl.program_id(0)
    tm = h3_ref.shape[0]
    n_assign = TOP_K * tm
    win = DMA_WINDOW

    @pl.when(step == 0)
    def _():
        zero_scr[...] = jnp.zeros_like(zero_scr)
        buf[...] = jnp.zeros_like(buf)

        def fill_copies(e):
            start = fill_start_ref[e]
            n = fill_n_ref[e]
            off = jnp.int32(0)
            out = []
            for size in _FILL_SIZES:
                take = (n & size) != 0
                out.append((take, pltpu.make_async_copy(zero_scr.at[pl.ds(0, size)],
                                                        xs_hbm.at[pl.ds(start + off, size)], row_sem)))
                off = off + jnp.where(take, size, 0)
            return out

        def issue(e, carry):
            for take, cp in fill_copies(e):
                @pl.when(take)
                def _():
                    cp.start()
            return carry

        def drain(e, carry):
            for take, cp in fill_copies(e):
                @pl.when(take)
                def _():
                    cp.wait()
            return carry

        lax.fori_loop(0, n_experts, issue, 0)
        lax.fori_loop(0, n_experts, drain, 0)

    @pl.when(step > 0)
    def _():
        tile = step - 1
        idx_copies = [pltpu.make_async_copy(lpos_hbm.at[tile], lpos_smem, idx_sem.at[0]),
                      pltpu.make_async_copy(win_hbm.at[tile], win_smem, idx_sem.at[1])]
        for cp in idx_copies:
            cp.start()
        for cp in idx_copies:
            cp.wait()

        def per_token(t, carry):
            row = h3_ref[t]
            for k in range(TOP_K):
                buf[lpos_smem[k * tm + t]] = row
            return carry

        lax.fori_loop(0, tm, per_token, 0, unroll=2)

        def window_copy(src, dst):
            return pltpu.make_async_copy(buf.at[pl.ds(src, win)], xs_hbm.at[pl.ds(dst, win)], row_sem)

        n_win_max = win_smem.shape[0] // SUBLANES
        n_win = win_smem[4 * n_win_max]

        def issue(q, carry):
            dst = pad_start_ref[win_smem[q]] + win_smem[n_win_max + q]
            window_copy(win_smem[2 * n_win_max + q], dst).start()
            return carry

        lax.fori_loop(0, n_win, issue, 0)
        _drain_windows(n_win, n_win_max, lambda slots: pltpu.make_async_copy(
            buf.at[pl.ds(0, slots)], xs_hbm.at[pl.ds(0, slots)], row_sem))


def _dispatch(h3, lposd, wins, fill_start, fill_n, pad_start, n_slots):
    m, ns, _ = h3.shape
    tm = MOE_TILE
    n_tiles = m // tm
    n_experts = fill_start.shape[0]
    win_words = wins.shape[1] * wins.shape[2]

    return pl.pallas_call(
        functools.partial(_dispatch_kernel, n_experts=n_experts),
        grid_spec=pltpu.PrefetchScalarGridSpec(
            num_scalar_prefetch=3,
            grid=(n_tiles + 1,),
            in_specs=[
                pl.BlockSpec(memory_space=pl.ANY),
                pl.BlockSpec(memory_space=pl.ANY),
                pl.BlockSpec((tm, ns, LANES), lambda i, fs, fn, ps: (jnp.maximum(i - 1, 0), 0, 0)),
            ],
            out_specs=pl.BlockSpec(memory_space=pl.ANY),
            scratch_shapes=[
                pltpu.SMEM((TOP_K * tm,), I32),
                pltpu.SMEM((win_words,), I32),
                pltpu.VMEM((TOP_K * tm + DMA_WINDOW, ns, LANES), U32),
                pltpu.VMEM((EXPERT_TILE, ns, LANES), U32),
                pltpu.SemaphoreType.DMA((2,)),
                pltpu.SemaphoreType.DMA,
            ],
        ),
        out_shape=jax.ShapeDtypeStruct((n_slots, ns, LANES), U32),
        compiler_params=_cparams("arbitrary"),
        name="moe_dispatch",
    )(fill_start, fill_n, pad_start, lposd.reshape(n_tiles, TOP_K * tm), wins.reshape(n_tiles, win_words), h3)


def _expert_kernel(be_ref, first_ref, nxt_ref, par_ref, na_ref, xs_ref, wg_hbm, wu_hbm, wd_hbm, y_ref,
                   wg_buf, wu_buf, wd_buf, wg_scr, wu_scr, wd_scr, sem, *, layer):
    i = pl.program_id(0)

    def fetch(expert, slot):
        return [pltpu.make_async_copy(src.at[layer, expert], dst.at[slot], sem.at[slot, j])
                for j, (src, dst) in enumerate(((wg_hbm, wg_buf), (wu_hbm, wu_buf), (wd_hbm, wd_buf)))]

    @pl.when(i < na_ref[0])
    def _():
        slot = par_ref[i]

        @pl.when(i == 0)
        def _():
            for cp in fetch(be_ref[0], slot):
                cp.start()

        @pl.when(first_ref[i] == 1)
        def _():
            @pl.when(nxt_ref[i] >= 0)
            def _():
                for cp in fetch(nxt_ref[i], 1 - slot):
                    cp.start()

            for cp in fetch(be_ref[i], slot):
                cp.wait()
            wg_scr[...] = wg_buf[slot].astype(BF16)
            wu_scr[...] = wu_buf[slot].astype(BF16)
            wd_scr[...] = wd_buf[slot].astype(BF16)

        x = _load_row_tiles_bf16(xs_ref, y_ref.shape[1])
        gate = _dot(x, wg_scr[...])
        up = _dot(x, wu_scr[...])
        act = (gate * _sigmoid(gate) * up).astype(BF16)
        _store_row_tiles(y_ref, _pack_bf16_pairs(_dot(act, wd_scr[...])))


def _experts(xs3, sched, w_gate, w_up, w_down, layer):
    n_slots, ns, _ = xs3.shape
    tm = EXPERT_TILE
    d, de = w_gate.shape[-2:]
    n_blocks = n_slots // tm

    def blk(i, be, first, nxt, par, na):
        return (jnp.minimum(i, na[0] - 1), 0, 0)

    return pl.pallas_call(
        functools.partial(_expert_kernel, layer=layer),
        grid_spec=pltpu.PrefetchScalarGridSpec(
            num_scalar_prefetch=5,
            grid=(n_blocks,),
            in_specs=[
                pl.BlockSpec((tm * ns, LANES), lambda *a: blk(*a)[:2]),
                pl.BlockSpec(memory_space=pl.ANY),
                pl.BlockSpec(memory_space=pl.ANY),
                pl.BlockSpec(memory_space=pl.ANY),
            ],
            out_specs=pl.BlockSpec((tm, ns, LANES), blk),
            scratch_shapes=[
                pltpu.VMEM((2, d, de), F32),
                pltpu.VMEM((2, d, de), F32),
                pltpu.VMEM((2, de, d), F32),
                pltpu.VMEM((d, de), BF16),
                pltpu.VMEM((d, de), BF16),
                pltpu.VMEM((de, d), BF16),
                pltpu.SemaphoreType.DMA((2, 3)),
            ],
        ),
        out_shape=jax.ShapeDtypeStruct((n_slots, ns, LANES), U32),
        compiler_params=_cparams("arbitrary"),
        name="moe_experts",
    )(*sched, xs3.reshape(n_slots * ns, LANES), w_gate, w_up, w_down)


def _combine_kernel(pad_start_ref, lpos_hbm, w_hbm, win_hbm, y_hbm, x_ref, h3_ref, g_ref, sg_ref, su_ref, sd_ref,
                    o_ref, lpos_smem, w_smem, win_smem, ybuf, rout, idx_sem, row_sem):
    step = pl.program_id(0)
    tm = x_ref.shape[0]
    ns = ybuf.shape[1]
    win = DMA_WINDOW
    idx_copies = [pltpu.make_async_copy(lpos_hbm.at[step], lpos_smem, idx_sem.at[0]),
                  pltpu.make_async_copy(w_hbm.at[step], w_smem, idx_sem.at[1]),
                  pltpu.make_async_copy(win_hbm.at[step], win_smem, idx_sem.at[2])]
    for cp in idx_copies:
        cp.start()

    @pl.when(step == 0)
    def _():
        ybuf[...] = jnp.zeros_like(ybuf)

    for cp in idx_copies:
        cp.wait()

    def window_copy(src, dst):
        return pltpu.make_async_copy(y_hbm.at[pl.ds(src, win)], ybuf.at[pl.ds(dst, win)], row_sem)

    n_win_max = win_smem.shape[0] // SUBLANES
    n_win = win_smem[4 * n_win_max]

    def issue(q, carry):
        src = pad_start_ref[win_smem[q]] + win_smem[n_win_max + q]
        window_copy(src, win_smem[3 * n_win_max + q]).start()
        return carry

    lax.fori_loop(0, n_win, issue, 0)

    hb = _load_row_tiles_bf16(h3_ref, ns)
    gate = _dot(hb, sg_ref[...])
    up = _dot(hb, su_ref[...])
    shared = _dot((gate * _sigmoid(gate) * up).astype(BF16), sd_ref[...])

    _drain_windows(n_win, n_win_max, lambda slots: pltpu.make_async_copy(
        y_hbm.at[pl.ds(0, slots)], ybuf.at[pl.ds(0, slots)], row_sem))

    def per_token(t, carry):
        acc_hi = jnp.zeros((ns, LANES), F32)
        acc_lo = jnp.zeros((ns, LANES), F32)
        for k in range(TOP_K):
            words = ybuf[lpos_smem[k * tm + t]]
            wk = w_smem[k * tm + t]
            acc_hi = acc_hi + wk * _unpack_hi(words)
            acc_lo = acc_lo + wk * _unpack_lo(words)
        rout[pl.ds(pl.multiple_of(t * 2 * ns, ns), ns), :] = acc_hi
        rout[pl.ds(pl.multiple_of(t * 2 * ns + ns, ns), ns), :] = acc_lo
        return carry

    lax.fori_loop(0, tm, per_token, 0, unroll=2)
    routed = jnp.concatenate([rout[pl.ds(s, tm, stride=2 * ns), :] for s in range(2 * ns)], axis=1)
    o_ref[...] = x_ref[...] + g_ref[0] * (routed + shared)


def _combine(lposp, w_sel, wins, pad_start, y3, x, h3, gate, s_gate, s_up, s_down, geom, layer):
    m, d = x.shape
    tm = MOE_TILE
    n_tiles = m // tm
    de = s_gate.shape[-1]
    ns = y3.shape[1]
    n_experts = pad_start.shape[0]
    n_assign = TOP_K * tm
    win_words = wins.shape[1] * wins.shape[2]
    midx = functools.partial(_mod_index, n_prompt_tiles=geom["mp"] // tm,
                             tiles_per_latent_seq=geom["ls"] // tm)
    return pl.pallas_call(
        _combine_kernel,
        grid_spec=pltpu.PrefetchScalarGridSpec(
            num_scalar_prefetch=1,
            grid=(n_tiles,),
            in_specs=[
                pl.BlockSpec(memory_space=pl.ANY),
                pl.BlockSpec(memory_space=pl.ANY),
                pl.BlockSpec(memory_space=pl.ANY),
                pl.BlockSpec(memory_space=pl.ANY),
                pl.BlockSpec((tm, d), lambda i, ps: (i, 0)),
                pl.BlockSpec((tm * ns, LANES), lambda i, ps: (i, 0)),
                pl.BlockSpec((1, 1, d), lambda i, ps: (midx(i), 0, 0)),
                pl.BlockSpec((None, d, de), lambda i, ps: (layer, 0, 0)),
                pl.BlockSpec((None, d, de), lambda i, ps: (layer, 0, 0)),
                pl.BlockSpec((None, de, d), lambda i, ps: (layer, 0, 0)),
            ],
            out_specs=pl.BlockSpec((tm, d), lambda i, ps: (i, 0)),
            scratch_shapes=[
                pltpu.SMEM((n_assign,), I32),
                pltpu.SMEM((n_assign,), F32),
                pltpu.SMEM((win_words,), I32),
                pltpu.VMEM((n_assign + n_experts * (DMA_WINDOW - 1), ns, LANES), U32),
                pltpu.VMEM((tm * 2 * ns, LANES), F32),
                pltpu.SemaphoreType.DMA((3,)),
                pltpu.SemaphoreType.DMA,
            ],
        ),
        out_shape=jax.ShapeDtypeStruct((m, d), F32),
        compiler_params=_cparams("arbitrary"),
        name="moe_combine",
    )(pad_start, lposp.reshape(n_tiles, n_assign), w_sel.reshape(n_tiles, n_assign),
      wins.reshape(n_tiles, win_words), y3, x, h3.reshape(m * ns, LANES), gate, s_gate, s_up, s_down)


def _moe_layer(x, norm_g, sc, sh, gate, w_router, bias, w_gate, w_up, w_down, s_gate, s_up, s_down,
               geom, layer):
    m, d = x.shape
    n_experts = w_router.shape[1]
    tm = EXPERT_TILE
    spare = DMA_WINDOW - 1
    h3, lposd, lposp, w_sel, wins, tile_cnt, tile_base = _route(x, norm_g, sc, sh, w_router, bias, geom)

    counts = (tile_base[-1, 0] + tile_cnt[-1, 0]).astype(I32)
    padded = jnp.where(counts > 0, (counts + spare + tm - 1) // tm * tm, 0)
    pad_end = jnp.cumsum(padded)
    pad_start = pad_end - padded
    n_blocks = -(-(m * TOP_K + n_experts * (spare + tm - 1)) // tm)
    n_active = pad_end[-1] // tm
    def expert_at(slot_rows):
        return jnp.minimum(jnp.sum((pad_end[None, :] <= slot_rows[:, None]).astype(I32), axis=1), n_experts - 1)

    blocks = jnp.arange(n_blocks, dtype=I32)
    block_expert = expert_at(blocks * tm)
    prev_expert = jnp.concatenate([jnp.full((1,), -1, I32), block_expert[:-1]])
    first = (block_expert != prev_expert).astype(I32)
    parity = (jnp.cumsum(first) - 1) % 2
    run_end = jnp.sum(jnp.where(block_expert[:, None] == jnp.arange(n_experts, dtype=I32)[None, :],
                                pad_end[None, :], 0), axis=1)
    next_expert = jnp.where(run_end // tm < n_active, expert_at(run_end), -1)
    sched = (block_expert, first, next_expert.astype(I32), parity.astype(I32), n_active.astype(I32).reshape(1))
    pad_start = pad_start.astype(I32)

    xs3 = _dispatch(h3, lposd, wins, pad_start + counts, padded - counts, pad_start, n_blocks * tm)
    y3 = _experts(xs3, sched, w_gate, w_up, w_down, layer)
    return _combine(lposp, w_sel, wins, pad_start, y3, x, h3, gate, s_gate.astype(BF16), s_up.astype(BF16),
                    s_down.astype(BF16), geom, layer)


def kernel(x_prompt, x_sample, c, state_ssm_re, state_ssm_im, cache_k, cache_v, c_ctx, ada_w, ada_b, norm1_g, norm2_g, final_norm_g, s5_lam_re, s5_lam_im, s5_log_step, s5_b_re, s5_b_im, s5_c_re, s5_c_im, s5_d, s5_w_glu, attn_w_qkv, attn_b_qkv, attn_w_o, attn_sink, moe_w_router, moe_router_bias, moe_w_gate, moe_w_up, moe_w_down, moe_shared_gate, moe_shared_up, moe_shared_down):
    bp, lp, d = x_prompt.shape
    bs, ls, _ = x_sample.shape
    depth = ada_w.shape[0]
    n_kv = cache_k.shape[3]
    mp, ms = bp * lp, bs * ls
    geom = dict(bp=bp, lp=lp, bs=bs, ls=ls, mp=mp, d=d)
    assert lp % ROW_TILE == 0 and ls % ROW_TILE == 0 and bs in (4,) and bp % SUBLANES == 0
    assert (2 * n_kv * HEAD_DIM) <= d and d % (2 * n_kv * HEAD_DIM) == 0

    x = jnp.concatenate([x_prompt.reshape(mp, d), x_sample.reshape(ms, d)], axis=0).astype(F32)

    n_mod = -(-(bs + 1) // SUBLANES) * SUBLANES
    cvec = jnp.zeros((n_mod, d), F32).at[0].set(c_ctx.astype(F32)).at[1:bs + 1].set(c.astype(F32))
    mods = _ada_params(cvec, ada_w, ada_b).reshape(depth, n_mod, 6, 1, d)

    new_re, new_im, new_k, new_v = [], [], [], []
    for i in range(depth):
        j = i // N_MIXERS
        sh1, sc1, g1, sh2, sc2, g2 = (mods[i, :, part] for part in range(6))
        if i % N_MIXERS == 0:
            h = _norm_mod_call(x, norm1_g[i], sc1, sh1, geom, F32)
            params = (s5_lam_re[j], s5_lam_im[j], s5_log_step[j], s5_b_re[j], s5_b_im[j],
                      s5_c_re[j], s5_c_im[j], s5_d[j])
            g, st_re, st_im = _s5_mixer(h, state_ssm_re[:, j], state_ssm_im[:, j], params, geom)
            new_re.append(st_re)
            new_im.append(st_im)
            x = _mm_resid(g, s5_w_glu[j].astype(BF16), x, g1, geom, glu=True)
        else:
            h = _norm_mod_call(x, norm1_g[i], sc1, sh1, geom, BF16)
            qkv = _mm_bias(h, attn_w_qkv[j].astype(BF16), attn_b_qkv[j].astype(F32))
            sink = attn_sink[j].astype(F32)
            o = jnp.concatenate([
                _attn_prompt(qkv, sink, geom, n_kv),
                _attn_latent(qkv, cache_k[:, j], cache_v[:, j], sink, geom, n_kv)], axis=0)
            kw = n_kv * HEAD_DIM
            new_k.append(qkv[:mp, d:d + kw].reshape(bp, lp, n_kv, HEAD_DIM))
            new_v.append(qkv[:mp, d + kw:d + 2 * kw].reshape(bp, lp, n_kv, HEAD_DIM))
            x = _mm_resid(o, attn_w_o[j].astype(BF16), x, g1, geom, glu=False)
        x = _moe_layer(x, norm2_g[i], sc2, sh2, g2, moe_w_router[i], moe_router_bias[i],
                       moe_w_gate, moe_w_up, moe_w_down, moe_shared_gate, moe_shared_up, moe_shared_down,
                       geom, i)

    return (_final_norm(x, final_norm_g, 0, mp).reshape(bp, lp, d),
            _final_norm(x, final_norm_g, mp, ms).reshape(bs, ls, d),
            jnp.stack(new_re, axis=1), jnp.stack(new_im, axis=1),
            jnp.stack(new_k, axis=1), jnp.stack(new_v, axis=1))
```

```python
import functools
import math

import jax
import jax.numpy as jnp
from jax import lax
from jax.experimental import pallas as pl
from jax.experimental.pallas import tpu as pltpu

F32 = jnp.float32
BF16 = jnp.bfloat16
I32 = jnp.int32

GROUP_CH = 16
HEAD_DIM = 64
WINDOW = 128
ATTN_BLOCK = 128
GRID_W = 64
ROPE_BASE = 10000.0
TOP_K = 8
N_EXPERT_GROUPS = 8
TOPK_GROUPS = 4
ROUTED_SCALE = 2.5
RMS_EPS = 1e-6
NEG_INF = -1e30
N_MIXERS = 2

LANES = 128
SUBLANES = 8
VMEM_LIMIT_BYTES = 56 * 1024 * 1024

S5_CHUNK = 16
ROW_TILE = 256
EXPERT_TILE = 256
MOE_TILE = 256
DMA_WINDOW = 8
U32 = jnp.uint32

_NT = (((1,), (1,)), ((), ()))


def _cparams(*sem):
    return pltpu.CompilerParams(dimension_semantics=sem, vmem_limit_bytes=VMEM_LIMIT_BYTES)


def _dot(a, b):
    return jnp.dot(a, b, preferred_element_type=F32)


def _split(x):
    hi = x.astype(BF16)
    lo = (x - hi.astype(F32)).astype(BF16)
    return hi, lo


def _dot3(a, b):
    a_hi, a_lo = _split(a)
    b_hi, b_lo = _split(b)
    return _dot(a_hi, b_hi) + _dot(a_lo, b_hi) + _dot(a_hi, b_lo)


def _sigmoid(x):
    return 1.0 / (1.0 + jnp.exp(-x))


def _gelu_tanh(x):
    c = math.sqrt(2.0 / math.pi)
    return 0.5 * x * (1.0 + jnp.tanh(c * (x + 0.044715 * (x * x * x))))


def _mod_index(tile, n_prompt_tiles, tiles_per_latent_seq):
    return jnp.where(tile < n_prompt_tiles, 0, 1 + (tile - n_prompt_tiles) // tiles_per_latent_seq)


def _ada_kernel(c_ref, w_ref, b_ref, o_ref):
    c = c_ref[...]
    o_ref[0] = _dot3(c * _sigmoid(c), w_ref[0]) + b_ref[0]


def _ada_params(cvec, ada_w, ada_b):
    depth, d, n = ada_w.shape
    rows = cvec.shape[0]
    tn = min(512, n)
    return pl.pallas_call(
        _ada_kernel,
        grid=(depth, n // tn),
        in_specs=[
            pl.BlockSpec((rows, d), lambda l, j: (0, 0)),
            pl.BlockSpec((1, d, tn), lambda l, j: (l, 0, j)),
            pl.BlockSpec((1, 1, tn), lambda l, j: (l, 0, j)),
        ],
        out_specs=pl.BlockSpec((1, rows, tn), lambda l, j: (l, 0, j)),
        out_shape=jax.ShapeDtypeStruct((depth, rows, n), F32),
        compiler_params=_cparams("arbitrary", "arbitrary"),
        name="ada_params",
    )(cvec, ada_w, ada_b.reshape(depth, 1, n))


def _norm_mod(x, g, sc, sh):
    ms = jnp.mean(x * x, axis=-1, keepdims=True)
    return (x * lax.rsqrt(ms + RMS_EPS) * g) * (1.0 + sc) + sh


def _norm_mod_kernel(x_ref, g_ref, sc_ref, sh_ref, o_ref):
    o_ref[...] = _norm_mod(x_ref[...], g_ref[...], sc_ref[0], sh_ref[0]).astype(o_ref.dtype)


def _norm_mod_call(x, g, sc, sh, geom, out_dtype, row0=0, n_rows=None):
    m, d = x.shape
    n_rows = m if n_rows is None else n_rows
    tm = ROW_TILE
    t0 = row0 // tm
    midx = functools.partial(_mod_index, n_prompt_tiles=geom["mp"] // tm,
                             tiles_per_latent_seq=geom["ls"] // tm)
    return pl.pallas_call(
        _norm_mod_kernel,
        grid=(n_rows // tm,),
        in_specs=[
            pl.BlockSpec((tm, d), lambda i: (t0 + i, 0)),
            pl.BlockSpec((1, d), lambda i: (0, 0)),
            pl.BlockSpec((1, 1, d), lambda i: (midx(t0 + i), 0, 0)),
            pl.BlockSpec((1, 1, d), lambda i: (midx(t0 + i), 0, 0)),
        ],
        out_specs=pl.BlockSpec((tm, d), lambda i: (i, 0)),
        out_shape=jax.ShapeDtypeStruct((n_rows, d), out_dtype),
        compiler_params=_cparams("arbitrary"),
        name="norm_mod",
    )(x, g.reshape(1, d), sc, sh)


def _final_norm_kernel(x_ref, g_ref, o_ref):
    x = x_ref[...]
    ms = jnp.mean(x * x, axis=-1, keepdims=True)
    o_ref[...] = x * lax.rsqrt(ms + RMS_EPS) * g_ref[...]


def _final_norm(x, g, row0, n_rows):
    d = x.shape[1]
    tm = ROW_TILE
    return pl.pallas_call(
        _final_norm_kernel,
        grid=(n_rows // tm,),
        in_specs=[
            pl.BlockSpec((tm, d), lambda i: (row0 // tm + i, 0)),
            pl.BlockSpec((1, d), lambda i: (0, 0)),
        ],
        out_specs=pl.BlockSpec((tm, d), lambda i: (i, 0)),
        out_shape=jax.ShapeDtypeStruct((n_rows, d), F32),
        compiler_params=_cparams("arbitrary"),
        name="final_norm",
    )(x, g.astype(F32).reshape(1, d))


def _mm_bias_kernel(x_ref, w_ref, b_ref, o_ref):
    o_ref[...] = _dot(x_ref[...], w_ref[...]) + b_ref[...]


def _mm_bias(x, w, b):
    m, k = x.shape
    n = w.shape[1]
    tm = min(512, m)
    tn = n // 2 if (n // 2) % LANES == 0 else min(512, n)
    return pl.pallas_call(
        _mm_bias_kernel,
        grid=(n // tn, m // tm),
        in_specs=[
            pl.BlockSpec((tm, k), lambda j, i: (i, 0)),
            pl.BlockSpec((k, tn), lambda j, i: (0, j)),
            pl.BlockSpec((1, tn), lambda j, i: (0, j)),
        ],
        out_specs=pl.BlockSpec((tm, tn), lambda j, i: (i, j)),
        out_shape=jax.ShapeDtypeStruct((m, n), F32),
        compiler_params=_cparams("arbitrary", "arbitrary"),
        name="mm_bias",
    )(x, w, b.reshape(1, n))


def _mm_resid_kernel(a_ref, w_ref, x_ref, g_ref, o_ref):
    o_ref[...] = x_ref[...] + g_ref[0] * _dot(a_ref[...], w_ref[...])


def _mm_glu_resid_kernel(a_ref, wv_ref, wg_ref, x_ref, g_ref, o_ref):
    a = a_ref[...].astype(BF16)
    val = _dot(a, wv_ref[...])
    gate = _dot(a, wg_ref[...])
    o_ref[...] = x_ref[...] + g_ref[0] * (val * _sigmoid(gate))


def _mm_resid(a, w, x, gate, geom, glu):
    m, k = a.shape
    n = x.shape[1]
    tm = math.gcd(geom["mp"], geom["ls"], 2 * ROW_TILE)
    tn = min(1024, n)
    nj = n // tn
    midx = functools.partial(_mod_index, n_prompt_tiles=geom["mp"] // tm,
                             tiles_per_latent_seq=geom["ls"] // tm)
    a_spec = pl.BlockSpec((tm, k), lambda j, i: (i, 0))
    w_spec = pl.BlockSpec((k, tn), lambda j, i: (0, j))
    x_spec = pl.BlockSpec((tm, tn), lambda j, i: (i, j))
    g_spec = pl.BlockSpec((1, 1, tn), lambda j, i: (midx(i), 0, j))
    if glu:
        body = _mm_glu_resid_kernel
        in_specs = [a_spec, w_spec, pl.BlockSpec((k, tn), lambda j, i: (0, j + nj)), x_spec, g_spec]
        args = (a, w, w, x, gate)
    else:
        body = _mm_resid_kernel
        in_specs = [a_spec, w_spec, x_spec, g_spec]
        args = (a, w, x, gate)
    return pl.pallas_call(
        body,
        grid=(nj, m // tm),
        in_specs=in_specs,
        out_specs=pl.BlockSpec((tm, tn), lambda j, i: (i, j)),
        out_shape=jax.ShapeDtypeStruct((m, n), F32),
        compiler_params=_cparams("arbitrary", "arbitrary"),
        name="mm_glu_resid" if glu else "mm_resid",
    )(*args)


def _s5_operands(lam_re, lam_im, log_step, b_re, b_im, c_re, c_im, d_skip):
    t = S5_CHUNK
    n_groups, n_state = lam_re.shape[1:]
    ch = GROUP_CH
    tc = t * ch
    lam = lax.complex(jnp.minimum(lam_re.astype(F32), -1e-4), lam_im.astype(F32))
    step = jnp.exp(log_step.astype(F32))[..., None]
    lam_bar = jnp.exp(lam * step)
    b_bar = ((lam_bar - 1.0) / lam)[..., None] * lax.complex(b_re.astype(F32), b_im.astype(F32))
    c_mat = lax.complex(c_re.astype(F32), c_im.astype(F32))
    pw = [jnp.ones_like(lam_bar)]
    for _ in range(t):
        pw.append(pw[-1] * lam_bar)
    pw = jnp.stack(pw, axis=1)

    w1_f = jnp.einsum("tgp,gph->gthp", pw[0, t - 1::-1][:t], b_bar[0]).reshape(n_groups, tc, n_state)
    w1_b = jnp.einsum("tgp,gph->gthp", pw[1, :t], b_bar[1]).reshape(n_groups, tc, n_state)
    zeros = jnp.zeros((n_groups, tc, n_state), F32)

    def pair_cols(m, gi):
        return jnp.concatenate([m, zeros] if gi == 0 else [zeros, m], axis=-1)

    def w1_for(gi):
        sel = slice(gi, None, 2)
        return jnp.concatenate([
            pair_cols(jnp.real(w1_f), gi)[sel], pair_cols(jnp.imag(w1_f), gi)[sel],
            pair_cols(jnp.real(w1_b), gi)[sel], pair_cols(jnp.imag(w1_b), gi)[sel]], axis=-1)

    w1 = jnp.stack([w1_for(0), w1_for(1)], axis=1)

    k_f = jnp.real(jnp.einsum("gop,kgp,gpi->gkio", c_mat[0], pw[0, :t], b_bar[0]))
    k_b = jnp.real(jnp.einsum("gop,kgp,gpi->gkio", c_mat[1], pw[1, :t], b_bar[1]))
    s_idx = jnp.arange(t)[:, None]
    t_idx = jnp.arange(t)[None, :]
    lag_f = t_idx - s_idx
    lag_b = s_idx - t_idx
    toep_f = jnp.where((lag_f >= 0)[None, :, :, None, None], k_f[:, jnp.clip(lag_f, 0, t - 1)], 0.0)
    toep_b = jnp.where((lag_b >= 0)[None, :, :, None, None], k_b[:, jnp.clip(lag_b, 0, t - 1)], 0.0)
    eye_t = jnp.eye(t, dtype=F32)[None, :, :, None, None]
    skip = eye_t * (d_skip.astype(F32).reshape(n_groups, ch)[:, None, None, :, None]
                    * jnp.eye(ch, dtype=F32)[None, None, None])
    toep = (toep_f + toep_b + skip).transpose(0, 1, 3, 2, 4).reshape(n_groups, tc, tc)
    cf = jnp.einsum("gop,tgp->gpto", c_mat[0], pw[0, 1:t + 1]).reshape(n_groups, n_state, tc)
    cb = jnp.einsum("gop,tgp->gpto", c_mat[1], pw[1, t:0:-1]).reshape(n_groups, n_state, tc)
    zrow = jnp.zeros((n_groups, n_state, tc), F32)

    def pair_rows(m):
        even = jnp.concatenate([m, zrow], axis=1)
        odd = jnp.concatenate([zrow, m], axis=1)
        return jnp.where((jnp.arange(n_groups) % 2 == 0)[:, None, None], even, odd)

    w3 = jnp.concatenate([toep, pair_rows(jnp.real(cf)), pair_rows(-jnp.imag(cf)),
                          pair_rows(jnp.real(cb)), pair_rows(-jnp.imag(cb))], axis=1).astype(BF16)

    a_pow = pw[:, t]
    a_vec = jnp.stack([jnp.real(a_pow[0]), jnp.imag(a_pow[0]), jnp.real(a_pow[1]), jnp.imag(a_pow[1])],
                      axis=1).reshape(n_groups // 2, 2, 4, n_state)
    a_vec = a_vec.transpose(0, 2, 1, 3).reshape(n_groups // 2, 4, 2 * n_state)
    return w1, w3, a_vec


def _complex_step(a_re, a_im, h_re, h_im, s_re, s_im):
    return a_re * h_re - a_im * h_im + s_re, a_re * h_im + a_im * h_re + s_im


def _s5_scan(s_scr, h_scr, a, init, n_chunks, rows_per_chunk):
    pl_ = LANES
    if rows_per_chunk % SUBLANES == 0:
        rt = rows_per_chunk

        def body(c, carry):
            f_re, f_im, g_re, g_im = carry
            rf = pl.ds(pl.multiple_of(c * rt, SUBLANES), rt)
            rb = pl.ds(pl.multiple_of((n_chunks - 1 - c) * rt, SUBLANES), rt)
            h_scr[0, rf, :] = f_re
            h_scr[1, rf, :] = f_im
            h_scr[2, rb, :] = g_re
            h_scr[3, rb, :] = g_im
            f_re, f_im = _complex_step(a[0:1], a[1:2], f_re, f_im,
                                       s_scr[rf, 0:pl_], s_scr[rf, pl_:2 * pl_])
            g_re, g_im = _complex_step(a[2:3], a[3:4], g_re, g_im,
                                       s_scr[rb, 2 * pl_:3 * pl_], s_scr[rb, 3 * pl_:4 * pl_])
            return f_re, f_im, g_re, g_im

        return lax.fori_loop(0, n_chunks, body, tuple(init))

    assert rows_per_chunk * 2 == SUBLANES and n_chunks % 2 == 0
    half = rows_per_chunk
    n_tiles = n_chunks // 2
    low = lax.broadcasted_iota(I32, (SUBLANES, pl_), 0) < half

    def body(j, carry):
        f_re, f_im, g_re, g_im = carry
        rf = pl.ds(pl.multiple_of(j * SUBLANES, SUBLANES), SUBLANES)
        rb = pl.ds(pl.multiple_of((n_tiles - 1 - j) * SUBLANES, SUBLANES), SUBLANES)
        sf_re, sf_im = s_scr[rf, 0:pl_], s_scr[rf, pl_:2 * pl_]
        sb_re, sb_im = s_scr[rb, 2 * pl_:3 * pl_], s_scr[rb, 3 * pl_:4 * pl_]
        f1_re, f1_im = _complex_step(a[0:1], a[1:2], f_re, f_im, sf_re, sf_im)
        g1_re, g1_im = _complex_step(a[2:3], a[3:4], g_re, g_im, sb_re, sb_im)
        f1s_re, f1s_im = pltpu.roll(f1_re, half, 0), pltpu.roll(f1_im, half, 0)
        g1s_re, g1s_im = pltpu.roll(g1_re, half, 0), pltpu.roll(g1_im, half, 0)
        h_scr[0, rf, :] = jnp.where(low, f_re, f1s_re)
        h_scr[1, rf, :] = jnp.where(low, f_im, f1s_im)
        h_scr[2, rb, :] = jnp.where(low, g1s_re, g_re)
        h_scr[3, rb, :] = jnp.where(low, g1s_im, g_im)
        f2_re, f2_im = _complex_step(a[0:1], a[1:2], f1s_re, f1s_im, sf_re, sf_im)
        g2_re, g2_im = _complex_step(a[2:3], a[3:4], g1s_re, g1s_im, sb_re, sb_im)
        return (pltpu.roll(f2_re, half, 0), pltpu.roll(f2_im, half, 0),
                pltpu.roll(g2_re, half, 0), pltpu.roll(g2_im, half, 0))

    return lax.fori_loop(0, n_tiles, body, tuple(init))


def _s5_kernel(up_ref, us_ref, w1_ref, w3_ref, a_ref, h0_ref, gp_ref, gs_ref, fin_ref,
               sp_scr, ss_scr, hp_scr, hs_scr, *, geom):
    tc = up_ref.shape[-1]
    a = a_ref[0]
    w1 = [w1_ref[0, 0], w1_ref[0, 1]]
    w1_hi = [w.astype(BF16) for w in w1]

    sp_scr[...] = _dot3(up_ref[0], w1[0]) + _dot3(up_ref[1], w1[1])
    ss_scr[...] = _dot(us_ref[0].astype(BF16), w1_hi[0]) + _dot(us_ref[1].astype(BF16), w1_hi[1])

    zero = jnp.zeros((geom["bp"], LANES), F32)
    fin = _s5_scan(sp_scr, hp_scr, a, (zero, zero, zero, zero), geom["lp"] // S5_CHUNK, geom["bp"])
    for comp in range(4):
        fin_ref[0, comp] = fin[comp]
    _s5_scan(ss_scr, hs_scr, a, tuple(h0_ref[0, comp] for comp in range(4)),
             geom["ls"] // S5_CHUNK, geom["bs"])

    for u_ref, h_scr, o_ref in ((up_ref, hp_scr, gp_ref), (us_ref, hs_scr, gs_ref)):
        for gi in range(2):
            y = _dot(u_ref[gi].astype(BF16), w3_ref[gi, 0:tc, :])
            for comp in range(4):
                y += _dot(h_scr[comp].astype(BF16),
                          w3_ref[gi, tc + comp * LANES:tc + (comp + 1) * LANES, :])
            o_ref[gi] = _gelu_tanh(y).astype(o_ref.dtype)


def _s5_mixer(h_prompt, h_latent, h0_re, h0_im, params, geom):
    d = h_prompt.shape[1]
    m = h_prompt.shape[0] + h_latent.shape[0]
    t = S5_CHUNK
    ch = GROUP_CH
    n_groups = d // ch
    tc = t * ch
    bp, lp, bs, ls, mp = geom["bp"], geom["lp"], geom["bs"], geom["ls"], geom["mp"]
    rp, rs = mp // t, (m - mp) // t
    w1, w3, a_vec = _s5_operands(*params)
    n_state = a_vec.shape[-1] // 2

    def to_chunks(x, b, l):
        x = x.reshape(b, l // t, t, n_groups, ch).transpose(3, 1, 0, 2, 4)
        return x.reshape(n_groups, (l // t) * b, tc)

    def from_chunks(x, b, l):
        x = x.reshape(n_groups, l // t, b, t, ch).transpose(2, 1, 3, 0, 4)
        return x.reshape(b * l, d)

    up = to_chunks(h_prompt, bp, lp)
    us = to_chunks(h_latent, bs, ls)
    h0 = jnp.stack([h0_re[:, 0], h0_im[:, 0], h0_re[:, 1], h0_im[:, 1]], axis=0).astype(F32)
    h0 = h0.reshape(4, bs, n_groups // 2, 2 * n_state).transpose(2, 0, 1, 3)
    reps = max(1, SUBLANES // bs)
    h0 = jnp.tile(h0, (1, 1, reps, 1))
    h0_rows = h0.shape[2]

    gp, gs, fin = pl.pallas_call(
        functools.partial(_s5_kernel, geom=geom),
        grid=(n_groups // 2,),
        in_specs=[
            pl.BlockSpec((2, rp, tc), lambda i: (i, 0, 0)),
            pl.BlockSpec((2, rs, tc), lambda i: (i, 0, 0)),
            pl.BlockSpec((1, 2, tc, 4 * LANES), lambda i: (i, 0, 0, 0)),
            pl.BlockSpec((2, tc + 4 * LANES, tc), lambda i: (i, 0, 0)),
            pl.BlockSpec((1, 4, LANES), lambda i: (i, 0, 0)),
            pl.BlockSpec((1, 4, h0_rows, LANES), lambda i: (i, 0, 0, 0)),
        ],
        out_specs=[
            pl.BlockSpec((2, rp, tc), lambda i: (i, 0, 0)),
            pl.BlockSpec((2, rs, tc), lambda i: (i, 0, 0)),
            pl.BlockSpec((1, 4, bp, LANES), lambda i: (i, 0, 0, 0)),
        ],
        out_shape=[
            jax.ShapeDtypeStruct((n_groups, rp, tc), BF16),
            jax.ShapeDtypeStruct((n_groups, rs, tc), BF16),
            jax.ShapeDtypeStruct((n_groups // 2, 4, bp, LANES), F32),
        ],
        scratch_shapes=[
            pltpu.VMEM((rp, 4 * LANES), F32),
            pltpu.VMEM((rs, 4 * LANES), F32),
            pltpu.VMEM((4, rp, LANES), F32),
            pltpu.VMEM((4, rs, LANES), F32),
        ],
        compiler_params=_cparams("arbitrary"),
        name="s5_mixer",
    )(up, us, w1, w3, a_vec, h0)

    g = jnp.concatenate([from_chunks(gp, bp, lp), from_chunks(gs, bs, ls)], axis=0)
    fin = fin.reshape(n_groups // 2, 2, 2, bp, 2, n_state).transpose(3, 1, 2, 0, 4, 5)
    fin = fin.reshape(bp, 2, 2, n_groups, n_state)
    return g, fin[:, :, 0], fin[:, :, 1]


def _softmax_with_sink(scores, sink):
    mx = sink
    for s in scores:
        mx = jnp.maximum(mx, jnp.max(s, axis=-1, keepdims=True))
    ps = [jnp.exp(s - mx) for s in scores]
    den = jnp.exp(sink - mx)
    for p in ps:
        den = den + jnp.sum(p, axis=-1, keepdims=True)
    return ps, 1.0 / den


def _attn_prompt_kernel(sink_ref, q_ref, kv_ref, o_ref, *, n_kv, q_per_kv):
    hd = HEAD_DIM
    scale = hd ** -0.5
    for kv in range(n_kv):
        k = kv_ref[:, kv * hd:(kv + 1) * hd].astype(BF16)
        v = kv_ref[:, (n_kv + kv) * hd:(n_kv + kv + 1) * hd].astype(BF16)
        for gq in range(q_per_kv):
            head = kv * q_per_kv + gq
            q = (q_ref[:, head * hd:(head + 1) * hd] * scale).astype(BF16)
            s = lax.dot_general(q, k, _NT, preferred_element_type=F32)
            (p,), inv = _softmax_with_sink([s], sink_ref[head])
            o_ref[:, head * hd:(head + 1) * hd] = (_dot(p.astype(BF16), v) * inv).astype(o_ref.dtype)


def _attn_prompt(qkv, sink, geom, n_kv):
    d = geom["d"]
    bp, lp = geom["bp"], geom["lp"]
    kvw = 2 * n_kv * HEAD_DIM
    n_heads = d // HEAD_DIM
    return pl.pallas_call(
        functools.partial(_attn_prompt_kernel, n_kv=n_kv, q_per_kv=n_heads // n_kv),
        grid=(bp,),
        in_specs=[
            pl.BlockSpec(memory_space=pltpu.SMEM),
            pl.BlockSpec((lp, d), lambda b: (b, 0)),
            pl.BlockSpec((lp, kvw), lambda b: (b, d // kvw)),
        ],
        out_specs=pl.BlockSpec((lp, d), lambda b: (b, 0)),
        out_shape=jax.ShapeDtypeStruct((bp * lp, d), BF16),
        compiler_params=_cparams("arbitrary"),
        name="attn_prompt",
    )(sink, qkv, qkv)


def _rope_tables(length):
    pairs = HEAD_DIM // 4
    pos = jnp.arange(length)
    row = (pos // GRID_W).astype(F32)
    col = (pos % GRID_W).astype(F32)
    inv_freq = ROPE_BASE ** (-jnp.arange(pairs, dtype=F32) / pairs)
    lane = jnp.arange(LANES) % HEAD_DIM
    axis_pos = jnp.where((lane // (2 * pairs))[None, :] == 0, row[:, None], col[:, None])
    ang = axis_pos * inv_freq[lane % pairs][None, :]
    sign = jnp.where((lane % (2 * pairs)) < pairs, -1.0, 1.0)[None, :]
    return jnp.cos(ang), jnp.sin(ang) * sign


def _rope(x, cos, sin):
    w = x.shape[1]
    pairs = HEAD_DIM // 4
    reps = w // LANES
    cos_w = jnp.concatenate([cos] * reps, axis=1) if reps > 1 else cos
    sin_w = jnp.concatenate([sin] * reps, axis=1) if reps > 1 else sin
    lane = lax.broadcasted_iota(I32, x.shape, 1)
    partner = jnp.where(lane % (2 * pairs) < pairs, pltpu.roll(x, w - pairs, 1), pltpu.roll(x, pairs, 1))
    return x * cos_w + partner * sin_w


def _attn_latent_kernel(sink_ref, q_ref, kvp_ref, kvo_ref, kvn_ref, ck_ref, cv_ref,
                        cos_p, sin_p, cos_o, sin_o, cos_n, sin_n, o_ref, *, n_kv, q_per_kv, n_blocks):
    hd = HEAD_DIM
    blk = ATTN_BLOCK
    scale = hd ** -0.5
    i = pl.program_id(1)
    kw = n_kv * hd
    k_win = jnp.concatenate([
        _rope(kvp_ref[:, 0:kw], cos_p[...], sin_p[...]),
        _rope(kvo_ref[:, 0:kw], cos_o[...], sin_o[...]),
        _rope(kvn_ref[:, 0:kw], cos_n[...], sin_n[...])], axis=0).astype(BF16)
    v_win = jnp.concatenate([kvp_ref[:, kw:2 * kw], kvo_ref[:, kw:2 * kw], kvn_ref[:, kw:2 * kw]],
                            axis=0).astype(BF16)
    rows = q_per_kv * blk
    r_in_blk = lax.broadcasted_iota(I32, (rows, 3 * blk), 0) % blk
    c_idx = lax.broadcasted_iota(I32, (rows, 3 * blk), 1)
    k_pos = (i - 1) * blk + c_idx
    mask = (jnp.abs(c_idx - blk - r_in_blk) <= WINDOW) & (k_pos >= 0) & (k_pos < n_blocks * blk)
    for kv in range(n_kv):
        qg = _rope(q_ref[:, kv * q_per_kv * hd:(kv + 1) * q_per_kv * hd], cos_o[...], sin_o[...]) * scale
        q = jnp.concatenate([qg[:, gq * hd:(gq + 1) * hd] for gq in range(q_per_kv)], axis=0).astype(BF16)
        sink = jnp.concatenate(
            [jnp.full((blk, 1), sink_ref[kv * q_per_kv + gq], F32) for gq in range(q_per_kv)], axis=0)
        s_ctx = lax.dot_general(q, ck_ref[0, :, kv * hd:(kv + 1) * hd].astype(BF16), _NT,
                                preferred_element_type=F32)
        s_win = lax.dot_general(q, k_win[:, kv * hd:(kv + 1) * hd], _NT, preferred_element_type=F32)
        s_win = jnp.where(mask, s_win, NEG_INF)
        (p_ctx, p_win), inv = _softmax_with_sink([s_ctx, s_win], sink)
        o = (_dot(p_ctx.astype(BF16), cv_ref[0, :, kv * hd:(kv + 1) * hd].astype(BF16))
             + _dot(p_win.astype(BF16), v_win[:, kv * hd:(kv + 1) * hd])) * inv
        for gq in range(q_per_kv):
            head = kv * q_per_kv + gq
            o_ref[:, head * hd:(head + 1) * hd] = o[gq * blk:(gq + 1) * blk].astype(o_ref.dtype)


def _attn_latent(qkv, ctx_k, ctx_v, sink, geom, n_kv):
    d = geom["d"]
    bs, ls, mp = geom["bs"], geom["ls"], geom["mp"]
    blk = ATTN_BLOCK
    nb = ls // blk
    kvw = 2 * n_kv * HEAD_DIM
    n_heads = d // HEAD_DIM
    past = ctx_k.shape[1]
    ck = ctx_k.reshape(bs, past, n_kv * HEAD_DIM).astype(F32)
    cv = ctx_v.reshape(bs, past, n_kv * HEAD_DIM).astype(F32)
    cos, sin = _rope_tables(ls)
    base = mp // blk

    def prev(i):
        return jnp.maximum(i - 1, 0)

    def nxt(i):
        return jnp.minimum(i + 1, nb - 1)

    def kv_spec(f):
        return pl.BlockSpec((blk, kvw), lambda b, i: (base + b * nb + f(i), d // kvw))

    def tab_spec(f):
        return pl.BlockSpec((blk, LANES), lambda b, i: (f(i), 0))

    same = lambda i: i
    return pl.pallas_call(
        functools.partial(_attn_latent_kernel, n_kv=n_kv, q_per_kv=n_heads // n_kv, n_blocks=nb),
        grid=(bs, nb),
        in_specs=[
            pl.BlockSpec(memory_space=pltpu.SMEM),
            pl.BlockSpec((blk, d), lambda b, i: (base + b * nb + i, 0)),
            kv_spec(prev), kv_spec(same), kv_spec(nxt),
            pl.BlockSpec((1, past, n_kv * HEAD_DIM), lambda b, i: (b, 0, 0)),
            pl.BlockSpec((1, past, n_kv * HEAD_DIM), lambda b, i: (b, 0, 0)),
            tab_spec(prev), tab_spec(prev), tab_spec(same), tab_spec(same), tab_spec(nxt), tab_spec(nxt),
        ],
        out_specs=pl.BlockSpec((blk, d), lambda b, i: (b * nb + i, 0)),
        out_shape=jax.ShapeDtypeStruct((bs * ls, d), BF16),
        compiler_params=_cparams("arbitrary", "arbitrary"),
        name="attn_latent",
    )(sink, qkv, qkv, qkv, qkv, ck, cv, cos, sin, cos, sin, cos, sin)


def _pack_bf16_pairs(x):
    n = x.shape[1] // 2

    def rounded_bits(v):
        return lax.bitcast_convert_type(v, U32) + jnp.uint32(0x8000)

    return (rounded_bits(x[:, :n]) & jnp.uint32(0xFFFF0000)) | (rounded_bits(x[:, n:]) >> 16)


def _store_row_tiles(ref, words):
    for s in range(ref.shape[1]):
        ref[:, s, :] = words[:, s * LANES:(s + 1) * LANES]


def _unpack_hi(words):
    return lax.bitcast_convert_type(words & jnp.uint32(0xFFFF0000), F32)


def _unpack_lo(words):
    return lax.bitcast_convert_type(words << 16, F32)


def _load_row_tiles_bf16(ref, ns):
    rows = ref.shape[0] // ns
    tiles = [ref[pl.ds(s, rows, stride=ns), :] for s in range(ns)]
    return jnp.concatenate([_unpack_hi(t).astype(BF16) for t in tiles]
                           + [_unpack_lo(t).astype(BF16) for t in tiles], axis=1)


def _router_kernel(x_ref, g_ref, sc_ref, sh_ref, wt_hi_ref, wt_lo_ref, bias_ref,
                   h3_ref, lposd_ref, lposp_ref, w_ref, win_ref, cnt_ref, base_ref, base_scr, base_col_scr,
                   *, n_experts):
    step = pl.program_id(0)

    @pl.when(step == 0)
    def _():
        base_scr[...] = jnp.zeros_like(base_scr)
        base_col_scr[...] = jnp.zeros_like(base_col_scr)

    h = _norm_mod(x_ref[...], g_ref[...], sc_ref[0], sh_ref[0])
    _store_row_tiles(h3_ref, _pack_bf16_pairs(h))
    tm = h.shape[0]
    h_hi, h_lo = _split(h)
    wt_hi = wt_hi_ref[...]
    logits = (lax.dot_general(wt_hi, h_hi, _NT, preferred_element_type=F32)
              + lax.dot_general(wt_lo_ref[...], h_hi, _NT, preferred_element_type=F32)
              + lax.dot_general(wt_hi, h_lo, _NT, preferred_element_type=F32))
    scores = _sigmoid(logits)
    sel = scores + bias_ref[:, 0:1]
    per_group = n_experts // N_EXPERT_GROUPS
    e_iota = lax.broadcasted_iota(I32, (n_experts, tm), 0)
    big = jnp.int32(1 << 30)

    grp = []
    for gidx in range(N_EXPERT_GROUPS):
        v = sel[gidx * per_group:(gidx + 1) * per_group]
        r = lax.broadcasted_iota(I32, v.shape, 0) + gidx * per_group
        m1 = jnp.max(v, axis=0, keepdims=True)
        i1 = jnp.min(jnp.where(v == m1, r, big), axis=0, keepdims=True)
        m2 = jnp.max(jnp.where(r == i1, -jnp.inf, v), axis=0, keepdims=True)
        grp.append(jnp.broadcast_to(m1 + m2, v.shape))
    cur = jnp.concatenate(grp, axis=0)
    g_iota = e_iota // per_group
    cand = jnp.full(sel.shape, -jnp.inf, F32)
    for _ in range(TOPK_GROUPS):
        mx = jnp.max(cur, axis=0, keepdims=True)
        gi = jnp.min(jnp.where(cur == mx, g_iota, big), axis=0, keepdims=True)
        hit = g_iota == gi
        cand = jnp.where(hit, sel, cand)
        cur = jnp.where(hit, -jnp.inf, cur)

    idxs, wts = [], []
    onehot = jnp.zeros((n_experts, tm), F32)
    for _ in range(TOP_K):
        mx = jnp.max(cand, axis=0, keepdims=True)
        ei = jnp.min(jnp.where(cand == mx, e_iota, big), axis=0, keepdims=True)
        hit = e_iota == ei
        idxs.append(ei)
        wts.append(jnp.sum(jnp.where(hit, scores, 0.0), axis=0, keepdims=True))
        onehot = jnp.where(hit, 1.0, onehot)
        cand = jnp.where(hit, -jnp.inf, cand)
    wsum = wts[0]
    for wk in wts[1:]:
        wsum = wsum + wk
    w_ref[0] = jnp.concatenate(wts, axis=0) / wsum * ROUTED_SCALE

    onehot_b = onehot.astype(BF16)
    t_row = lax.broadcasted_iota(I32, (tm, tm), 0)
    t_col = lax.broadcasted_iota(I32, (tm, tm), 1)
    earlier_tok = _dot(onehot_b, jnp.where(t_row < t_col, 1.0, 0.0).astype(BF16))
    cnt = jnp.sum(onehot, axis=1, keepdims=True)
    cnt_pad = jnp.floor((cnt + (DMA_WINDOW - 1)) * (1.0 / DMA_WINDOW)) * DMA_WINDOW
    e_row = lax.broadcasted_iota(I32, (n_experts, n_experts), 0)
    e_col = lax.broadcasted_iota(I32, (n_experts, n_experts), 1)
    lower = jnp.where(e_col < e_row, 1.0, 0.0).astype(BF16)
    start_d = _dot(lower, jnp.broadcast_to(cnt, (n_experts, LANES)).astype(BF16))[:, 0:1]
    start_p = _dot(lower, jnp.broadcast_to(cnt_pad, (n_experts, LANES)).astype(BF16))[:, 0:1]
    pos_d = start_d + earlier_tok
    pos_p = start_p + earlier_tok
    lposd_ref[0] = jnp.concatenate(
        [jnp.sum(jnp.where(e_iota == ei, pos_d, 0.0), axis=0, keepdims=True) for ei in idxs], axis=0).astype(I32)
    lposp_ref[0] = jnp.concatenate(
        [jnp.sum(jnp.where(e_iota == ei, pos_p, 0.0), axis=0, keepdims=True) for ei in idxs], axis=0).astype(I32)

    n_win_max = win_ref.shape[2]
    nw = cnt_pad * (1.0 / DMA_WINDOW)
    w_start = _dot(lower, jnp.broadcast_to(nw, (n_experts, LANES)).astype(BF16))[:, 0:1]
    q = lax.broadcasted_iota(I32, (n_experts, n_win_max), 1).astype(F32)
    w_expert = jnp.sum(jnp.where(w_start + nw <= q, 1.0, 0.0), axis=0, keepdims=True)
    mine = lax.broadcasted_iota(I32, (n_experts, n_win_max), 0).astype(F32) == w_expert

    def pick(col):
        return jnp.sum(jnp.where(mine, col, 0.0), axis=0, keepdims=True)

    q_row = lax.broadcasted_iota(I32, (1, n_win_max), 1).astype(F32)
    offset = (q_row - pick(w_start)) * DMA_WINDOW
    total = jnp.broadcast_to(jnp.sum(nw, axis=0, keepdims=True), (1, n_win_max))
    zero_row = jnp.zeros((1, n_win_max), F32)
    win_ref[0] = jnp.concatenate(
        [w_expert, pick(base_col_scr[:, 0:1]) + offset, pick(start_d) + offset, pick(start_p) + offset, total,
         zero_row, zero_row, zero_row], axis=0).astype(I32)
    base_col_scr[...] = base_col_scr[...] + cnt

    cnt_row = lax.dot_general(jnp.ones((SUBLANES, tm), BF16), onehot_b, _NT, preferred_element_type=F32)
    cnt_ref[0] = cnt_row
    base_ref[0] = base_scr[...]
    base_scr[...] = base_scr[...] + cnt_row


def _route(x, g, sc, sh, w_router, bias, geom):
    m, d = x.shape
    n_experts = w_router.shape[1]
    tm = MOE_TILE
    n_tiles = m // tm
    ns = d // (2 * LANES)
    wt = w_router.astype(F32).T
    wt_hi = wt.astype(BF16)
    wt_lo = (wt - wt_hi.astype(F32)).astype(BF16)
    bias_b = jnp.broadcast_to(bias.astype(F32)[:, None], (n_experts, LANES))
    midx = functools.partial(_mod_index, n_prompt_tiles=geom["mp"] // tm,
                             tiles_per_latent_seq=geom["ls"] // tm)
    n_win_max = TOP_K * tm // DMA_WINDOW + n_experts
    tok_spec = pl.BlockSpec((1, TOP_K, tm), lambda i: (i, 0, 0))
    win_spec = pl.BlockSpec((1, SUBLANES, n_win_max), lambda i: (i, 0, 0))
    exp_spec = pl.BlockSpec((1, SUBLANES, n_experts), lambda i: (i, 0, 0))
    return pl.pallas_call(
        functools.partial(_router_kernel, n_experts=n_experts),
        grid=(n_tiles,),
        in_specs=[
            pl.BlockSpec((tm, d), lambda i: (i, 0)),
            pl.BlockSpec((1, d), lambda i: (0, 0)),
            pl.BlockSpec((1, 1, d), lambda i: (midx(i), 0, 0)),
            pl.BlockSpec((1, 1, d), lambda i: (midx(i), 0, 0)),
            pl.BlockSpec((n_experts, d), lambda i: (0, 0)),
            pl.BlockSpec((n_experts, d), lambda i: (0, 0)),
            pl.BlockSpec((n_experts, LANES), lambda i: (0, 0)),
        ],
        out_specs=[
            pl.BlockSpec((tm, ns, LANES), lambda i: (i, 0, 0)),
            tok_spec, tok_spec, tok_spec, win_spec, exp_spec, exp_spec,
        ],
        out_shape=[
            jax.ShapeDtypeStruct((m, ns, LANES), U32),
            jax.ShapeDtypeStruct((n_tiles, TOP_K, tm), I32),
            jax.ShapeDtypeStruct((n_tiles, TOP_K, tm), I32),
            jax.ShapeDtypeStruct((n_tiles, TOP_K, tm), F32),
            jax.ShapeDtypeStruct((n_tiles, SUBLANES, n_win_max), I32),
            jax.ShapeDtypeStruct((n_tiles, SUBLANES, n_experts), F32),
            jax.ShapeDtypeStruct((n_tiles, SUBLANES, n_experts), F32),
        ],
        scratch_shapes=[pltpu.VMEM((SUBLANES, n_experts), F32), pltpu.VMEM((n_experts, LANES), F32)],
        compiler_params=_cparams("arbitrary"),
        name="moe_router",
    )(x, g.reshape(1, d), sc, sh, wt_hi, wt_lo, bias_b)


_FILL_SIZES = tuple(1 << b for b in range(EXPERT_TILE.bit_length() - 1, -1, -1))


def _drain_windows(n_win, n_win_max, copy_of_slots):
    max_chunk = TOP_K * MOE_TILE // DMA_WINDOW
    sizes = [max_chunk] * (n_win_max // max_chunk)
    size = max_chunk // 2
    while size >= 1:
        sizes.append(size)
        size //= 2
    left = n_win
    for s in sizes:
        take = left >= s

        @pl.when(take)
        def _():
            copy_of_slots(s * DMA_WINDOW).wait()

        left = left - jnp.where(take, s, 0)


def _dispatch_kernel(fill_start_ref, fill_n_ref, pad_start_ref, lpos_hbm, win_hbm, h3_ref, xs_hbm,
                     lpos_smem, win_smem, buf, zero_scr, idx_sem, row_sem, *, n_experts):
    step = pl.program_id(0)
    tm = h3_ref.shape[0]
    n_assign = TOP_K * tm
    win = DMA_WINDOW

    @pl.when(step == 0)
    def _():
        zero_scr[...] = jnp.zeros_like(zero_scr)
        buf[...] = jnp.zeros_like(buf)

        def fill_copies(e):
            start = fill_start_ref[e]
            n = fill_n_ref[e]
            off = jnp.int32(0)
            out = []
            for size in _FILL_SIZES:
                take = (n & size) != 0
                out.append((take, pltpu.make_async_copy(zero_scr.at[pl.ds(0, size)],
                                                        xs_hbm.at[pl.ds(start + off, size)], row_sem)))
                off = off + jnp.where(take, size, 0)
            return out

        def issue(e, carry):
            for take, cp in fill_copies(e):
                @pl.when(take)
                def _():
                    cp.start()
            return carry

        def drain(e, carry):
            for take, cp in fill_copies(e):
                @pl.when(take)
                def _():
                    cp.wait()
            return carry

        lax.fori_loop(0, n_experts, issue, 0)
        lax.fori_loop(0, n_experts, drain, 0)

    @pl.when(step > 0)
    def _():
        tile = step - 1
        idx_copies = [pltpu.make_async_copy(lpos_hbm.at[tile], lpos_smem, idx_sem.at[0]),
                      pltpu.make_async_copy(win_hbm.at[tile], win_smem, idx_sem.at[1])]
        for cp in idx_copies:
            cp.start()
        for cp in idx_copies:
            cp.wait()

        def per_token(t, carry):
            row = h3_ref[t]
            for k in range(TOP_K):
                buf[lpos_smem[k * tm + t]] = row
            return carry

        lax.fori_loop(0, tm, per_token, 0, unroll=2)

        def window_copy(src, dst):
            return pltpu.make_async_copy(buf.at[pl.ds(src, win)], xs_hbm.at[pl.ds(dst, win)], row_sem)

        n_win_max = win_smem.shape[0] // SUBLANES
        n_win = win_smem[4 * n_win_max]

        def issue(q, carry):
            dst = pad_start_ref[win_smem[q]] + win_smem[n_win_max + q]
            window_copy(win_smem[2 * n_win_max + q], dst).start()
            return carry

        lax.fori_loop(0, n_win, issue, 0)
        _drain_windows(n_win, n_win_max, lambda slots: pltpu.make_async_copy(
            buf.at[pl.ds(0, slots)], xs_hbm.at[pl.ds(0, slots)], row_sem))


def _dispatch(h3, lposd, wins, fill_start, fill_n, pad_start, n_slots):
    m, ns, _ = h3.shape
    tm = MOE_TILE
    n_tiles = m // tm
    n_experts = fill_start.shape[0]
    win_words = wins.shape[1] * wins.shape[2]

    return pl.pallas_call(
        functools.partial(_dispatch_kernel, n_experts=n_experts),
        grid_spec=pltpu.PrefetchScalarGridSpec(
            num_scalar_prefetch=3,
            grid=(n_tiles + 1,),
            in_specs=[
                pl.BlockSpec(memory_space=pl.ANY),
                pl.BlockSpec(memory_space=pl.ANY),
                pl.BlockSpec((tm, ns, LANES), lambda i, fs, fn, ps: (jnp.maximum(i - 1, 0), 0, 0)),
            ],
            out_specs=pl.BlockSpec(memory_space=pl.ANY),
            scratch_shapes=[
                pltpu.SMEM((TOP_K * tm,), I32),
                pltpu.SMEM((win_words,), I32),
                pltpu.VMEM((TOP_K * tm + DMA_WINDOW, ns, LANES), U32),
                pltpu.VMEM((EXPERT_TILE, ns, LANES), U32),
                pltpu.SemaphoreType.DMA((2,)),
                pltpu.SemaphoreType.DMA,
            ],
        ),
        out_shape=jax.ShapeDtypeStruct((n_slots, ns, LANES), U32),
        compiler_params=_cparams("arbitrary"),
        name="moe_dispatch",
    )(fill_start, fill_n, pad_start, lposd.reshape(n_tiles, TOP_K * tm), wins.reshape(n_tiles, win_words), h3)


def _expert_kernel(be_ref, first_ref, nxt_ref, par_ref, na_ref, xs_ref, wg_hbm, wu_hbm, wd_hbm, y_ref,
                   wg_buf, wu_buf, wd_buf, wg_scr, wu_scr, wd_scr, sem, *, layer):
    i = pl.program_id(0)

    def fetch(expert, slot):
        return [pltpu.make_async_copy(src.at[layer, expert], dst.at[slot], sem.at[slot, j])
                for j, (src, dst) in enumerate(((wg_hbm, wg_buf), (wu_hbm, wu_buf), (wd_hbm, wd_buf)))]

    @pl.when(i < na_ref[0])
    def _():
        slot = par_ref[i]

        @pl.when(i == 0)
        def _():
            for cp in fetch(be_ref[0], slot):
                cp.start()

        @pl.when(first_ref[i] == 1)
        def _():
            @pl.when(nxt_ref[i] >= 0)
            def _():
                for cp in fetch(nxt_ref[i], 1 - slot):
                    cp.start()

            for cp in fetch(be_ref[i], slot):
                cp.wait()
            wg_scr[...] = wg_buf[slot].astype(BF16)
            wu_scr[...] = wu_buf[slot].astype(BF16)
            wd_scr[...] = wd_buf[slot].astype(BF16)

        x = _load_row_tiles_bf16(xs_ref, y_ref.shape[1])
        gate = _dot(x, wg_scr[...])
        up = _dot(x, wu_scr[...])
        act = (gate * _sigmoid(gate) * up).astype(BF16)
        _store_row_tiles(y_ref, _pack_bf16_pairs(_dot(act, wd_scr[...])))


def _experts(xs3, sched, w_gate, w_up, w_down, layer):
    n_slots, ns, _ = xs3.shape
    tm = EXPERT_TILE
    d, de = w_gate.shape[-2:]
    n_blocks = n_slots // tm

    def blk(i, be, first, nxt, par, na):
        return (jnp.minimum(i, na[0] - 1), 0, 0)

    return pl.pallas_call(
        functools.partial(_expert_kernel, layer=layer),
        grid_spec=pltpu.PrefetchScalarGridSpec(
            num_scalar_prefetch=5,
            grid=(n_blocks,),
            in_specs=[
                pl.BlockSpec((tm * ns, LANES), lambda *a: blk(*a)[:2]),
                pl.BlockSpec(memory_space=pl.ANY),
                pl.BlockSpec(memory_space=pl.ANY),
                pl.BlockSpec(memory_space=pl.ANY),
            ],
            out_specs=pl.BlockSpec((tm, ns, LANES), blk),
            scratch_shapes=[
                pltpu.VMEM((2, d, de), F32),
                pltpu.VMEM((2, d, de), F32),
                pltpu.VMEM((2, de, d), F32),
                pltpu.VMEM((d, de), BF16),
                pltpu.VMEM((d, de), BF16),
                pltpu.VMEM((de, d), BF16),
                pltpu.SemaphoreType.DMA((2, 3)),
            ],
        ),
        out_shape=jax.ShapeDtypeStruct((n_slots, ns, LANES), U32),
        compiler_params=_cparams("arbitrary"),
        name="moe_experts",
    )(*sched, xs3.reshape(n_slots * ns, LANES), w_gate, w_up, w_down)


def _combine_kernel(pad_start_ref, lpos_hbm, w_hbm, win_hbm, y_hbm, x_ref, h3_ref, g_ref, sg_ref, su_ref, sd_ref,
                    o_ref, lpos_smem, w_smem, win_smem, ybuf, rout, idx_sem, row_sem):
    step = pl.program_id(0)
    tm = x_ref.shape[0]
    ns = ybuf.shape[1]
    win = DMA_WINDOW
    idx_copies = [pltpu.make_async_copy(lpos_hbm.at[step], lpos_smem, idx_sem.at[0]),
                  pltpu.make_async_copy(w_hbm.at[step], w_smem, idx_sem.at[1]),
                  pltpu.make_async_copy(win_hbm.at[step], win_smem, idx_sem.at[2])]
    for cp in idx_copies:
        cp.start()

    @pl.when(step == 0)
    def _():
        ybuf[...] = jnp.zeros_like(ybuf)

    for cp in idx_copies:
        cp.wait()

    def window_copy(src, dst):
        return pltpu.make_async_copy(y_hbm.at[pl.ds(src, win)], ybuf.at[pl.ds(dst, win)], row_sem)

    n_win_max = win_smem.shape[0] // SUBLANES
    n_win = win_smem[4 * n_win_max]

    def issue(q, carry):
        src = pad_start_ref[win_smem[q]] + win_smem[n_win_max + q]
        window_copy(src, win_smem[3 * n_win_max + q]).start()
        return carry

    lax.fori_loop(0, n_win, issue, 0)

    hb = _load_row_tiles_bf16(h3_ref, ns)
    gate = _dot(hb, sg_ref[...])
    up = _dot(hb, su_ref[...])
    shared = _dot((gate * _sigmoid(gate) * up).astype(BF16), sd_ref[...])

    _drain_windows(n_win, n_win_max, lambda slots: pltpu.make_async_copy(
        y_hbm.at[pl.ds(0, slots)], ybuf.at[pl.ds(0, slots)], row_sem))

    def per_token(t, carry):
        acc_hi = jnp.zeros((ns, LANES), F32)
        acc_lo = jnp.zeros((ns, LANES), F32)
        for k in range(TOP_K):
            words = ybuf[lpos_smem[k * tm + t]]
            wk = w_smem[k * tm + t]
            acc_hi = acc_hi + wk * _unpack_hi(words)
            acc_lo = acc_lo + wk * _unpack_lo(words)
        rout[pl.ds(pl.multiple_of(t * 2 * ns, ns), ns), :] = acc_hi
        rout[pl.ds(pl.multiple_of(t * 2 * ns + ns, ns), ns), :] = acc_lo
        return carry

    lax.fori_loop(0, tm, per_token, 0, unroll=2)
    routed = jnp.concatenate([rout[pl.ds(s, tm, stride=2 * ns), :] for s in range(2 * ns)], axis=1)
    o_ref[...] = x_ref[...] + g_ref[0] * (routed + shared)


def _combine(lposp, w_sel, wins, pad_start, y3, x, h3, gate, s_gate, s_up, s_down, geom, layer):
    m, d = x.shape
    tm = MOE_TILE
    n_tiles = m // tm
    de = s_gate.shape[-1]
    ns = y3.shape[1]
    n_experts = pad_start.shape[0]
    n_assign = TOP_K * tm
    win_words = wins.shape[1] * wins.shape[2]
    midx = functools.partial(_mod_index, n_prompt_tiles=geom["mp"] // tm,
                             tiles_per_latent_seq=geom["ls"] // tm)
    return pl.pallas_call(
        _combine_kernel,
        grid_spec=pltpu.PrefetchScalarGridSpec(
            num_scalar_prefetch=1,
            grid=(n_tiles,),
            in_specs=[
                pl.BlockSpec(memory_space=pl.ANY),
                pl.BlockSpec(memory_space=pl.ANY),
                pl.BlockSpec(memory_space=pl.ANY),
                pl.BlockSpec(memory_space=pl.ANY),
                pl.BlockSpec((tm, d), lambda i, ps: (i, 0)),
                pl.BlockSpec((tm * ns, LANES), lambda i, ps: (i, 0)),
                pl.BlockSpec((1, 1, d), lambda i, ps: (midx(i), 0, 0)),
                pl.BlockSpec((None, d, de), lambda i, ps: (layer, 0, 0)),
                pl.BlockSpec((None, d, de), lambda i, ps: (layer, 0, 0)),
                pl.BlockSpec((None, de, d), lambda i, ps: (layer, 0, 0)),
            ],
            out_specs=pl.BlockSpec((tm, d), lambda i, ps: (i, 0)),
            scratch_shapes=[
                pltpu.SMEM((n_assign,), I32),
                pltpu.SMEM((n_assign,), F32),
                pltpu.SMEM((win_words,), I32),
                pltpu.VMEM((n_assign + n_experts * (DMA_WINDOW - 1), ns, LANES), U32),
                pltpu.VMEM((tm * 2 * ns, LANES), F32),
                pltpu.SemaphoreType.DMA((3,)),
                pltpu.SemaphoreType.DMA,
            ],
        ),
        out_shape=jax.ShapeDtypeStruct((m, d), F32),
        compiler_params=_cparams("arbitrary"),
        name="moe_combine",
    )(pad_start, lposp.reshape(n_tiles, n_assign), w_sel.reshape(n_tiles, n_assign),
      wins.reshape(n_tiles, win_words), y3, x, h3.reshape(m * ns, LANES), gate, s_gate, s_up, s_down)


def _moe_layer(x, norm_g, sc, sh, gate, w_router, bias, w_gate, w_up, w_down, s_gate, s_up, s_down,
               geom, layer):
    m, d = x.shape
    n_experts = w_router.shape[1]
    tm = EXPERT_TILE
    spare = DMA_WINDOW - 1
    h3, lposd, lposp, w_sel, wins, tile_cnt, tile_base = _route(x, norm_g, sc, sh, w_router, bias, geom)

    counts = (tile_base[-1, 0] + tile_cnt[-1, 0]).astype(I32)
    padded = jnp.where(counts > 0, (counts + spare + tm - 1) // tm * tm, 0)
    pad_end = jnp.cumsum(padded)
    pad_start = pad_end - padded
    n_blocks = -(-(m * TOP_K + n_experts * (spare + tm - 1)) // tm)
    n_active = pad_end[-1] // tm
    def expert_at(slot_rows):
        return jnp.minimum(jnp.sum((pad_end[None, :] <= slot_rows[:, None]).astype(I32), axis=1), n_experts - 1)

    blocks = jnp.arange(n_blocks, dtype=I32)
    block_expert = expert_at(blocks * tm)
    prev_expert = jnp.concatenate([jnp.full((1,), -1, I32), block_expert[:-1]])
    first = (block_expert != prev_expert).astype(I32)
    parity = (jnp.cumsum(first) - 1) % 2
    run_end = jnp.sum(jnp.where(block_expert[:, None] == jnp.arange(n_experts, dtype=I32)[None, :],
                                pad_end[None, :], 0), axis=1)
    next_expert = jnp.where(run_end // tm < n_active, expert_at(run_end), -1)
    sched = (block_expert, first, next_expert.astype(I32), parity.astype(I32), n_active.astype(I32).reshape(1))
    pad_start = pad_start.astype(I32)

    xs3 = _dispatch(h3, lposd, wins, pad_start + counts, padded - counts, pad_start, n_blocks * tm)
    y3 = _experts(xs3, sched, w_gate, w_up, w_down, layer)
    return _combine(lposp, w_sel, wins, pad_start, y3, x, h3, gate, s_gate.astype(BF16), s_up.astype(BF16),
                    s_down.astype(BF16), geom, layer)


def kernel(x_prompt, x_sample, c, state_ssm_re, state_ssm_im, cache_k, cache_v, c_ctx, ada_w, ada_b, norm1_g, norm2_g, final_norm_g, s5_lam_re, s5_lam_im, s5_log_step, s5_b_re, s5_b_im, s5_c_re, s5_c_im, s5_d, s5_w_glu, attn_w_qkv, attn_b_qkv, attn_w_o, attn_sink, moe_w_router, moe_router_bias, moe_w_gate, moe_w_up, moe_w_down, moe_shared_gate, moe_shared_up, moe_shared_down):
    bp, lp, d = x_prompt.shape
    bs, ls, _ = x_sample.shape
    depth = ada_w.shape[0]
    n_kv = cache_k.shape[3]
    mp, ms = bp * lp, bs * ls
    geom = dict(bp=bp, lp=lp, bs=bs, ls=ls, mp=mp, d=d)
    assert lp % ROW_TILE == 0 and ls % ROW_TILE == 0 and bs in (4,) and bp % SUBLANES == 0
    assert (2 * n_kv * HEAD_DIM) <= d and d % (2 * n_kv * HEAD_DIM) == 0

    x = jnp.concatenate([x_prompt.reshape(mp, d), x_sample.reshape(ms, d)], axis=0).astype(F32)

    n_mod = -(-(bs + 1) // SUBLANES) * SUBLANES
    cvec = jnp.zeros((n_mod, d), F32).at[0].set(c_ctx.astype(F32)).at[1:bs + 1].set(c.astype(F32))
    mods = _ada_params(cvec, ada_w, ada_b).reshape(depth, n_mod, 6, 1, d)

    new_re, new_im, new_k, new_v = [], [], [], []
    for i in range(depth):
        j = i // N_MIXERS
        sh1, sc1, g1, sh2, sc2, g2 = (mods[i, :, part] for part in range(6))
        if i % N_MIXERS == 0:
            h_prompt = _norm_mod_call(x, norm1_g[i], sc1, sh1, geom, F32, 0, mp)
            h_latent = _norm_mod_call(x, norm1_g[i], sc1, sh1, geom, F32, mp, ms)
            params = (s5_lam_re[j], s5_lam_im[j], s5_log_step[j], s5_b_re[j], s5_b_im[j],
                      s5_c_re[j], s5_c_im[j], s5_d[j])
            g, st_re, st_im = _s5_mixer(h_prompt, h_latent, state_ssm_re[:, j], state_ssm_im[:, j], params, geom)
            new_re.append(st_re)
            new_im.append(st_im)
            x = _mm_resid(g, s5_w_glu[j].astype(BF16), x, g1, geom, glu=True)
        else:
            h = _norm_mod_call(x, norm1_g[i], sc1, sh1, geom, BF16)
            qkv = _mm_bias(h, attn_w_qkv[j].astype(BF16), attn_b_qkv[j].astype(F32))
            sink = attn_sink[j].astype(F32)
            o = jnp.concatenate([
                _attn_prompt(qkv, sink, geom, n_kv),
                _attn_latent(qkv, cache_k[:, j], cache_v[:, j], sink, geom, n_kv)], axis=0)
            kw = n_kv * HEAD_DIM
            new_k.append(qkv[:mp, d:d + kw].reshape(bp, lp, n_kv, HEAD_DIM))
            new_v.append(qkv[:mp, d + kw:d + 2 * kw].reshape(bp, lp, n_kv, HEAD_DIM))
            x = _mm_resid(o, attn_w_o[j].astype(BF16), x, g1, geom, glu=False)
        x = _moe_layer(x, norm2_g[i], sc2, sh2, g2, moe_w_router[i], moe_router_bias[i],
                       moe_w_gate, moe_w_up, moe_w_down, moe_shared_gate, moe_shared_up, moe_shared_down,
                       geom, i)

    return (_final_norm(x, final_norm_g, 0, mp).reshape(bp, lp, d),
            _final_norm(x, final_norm_g, mp, ms).reshape(bs, ls, d),
            jnp.stack(new_re, axis=1), jnp.stack(new_im, axis=1),
            jnp.stack(new_k, axis=1), jnp.stack(new_v, axis=1))
```
